```python
import jax, jax.numpy as jnp
from jax import lax
import numpy as np

D_MODEL = 2048
BATCH = 1
SEQ = 16384
DEPTH = 1

MLA_HEADS = 8
MLA_Q_RANK = 512
MLA_KV_RANK = 512
MLA_NOPE_DIM = 128
MLA_ROPE_DIM = 64
MLA_V_DIM = 128
FOX_HEADS = 8
FOX_HEAD_DIM = 128
D_FF = 5632
CONV_WIDTH = 3
Q_BLOCK = 128
ROPE_THETA = 10000.0
NORM_EPS = 1e-6

IN_SPLITS = (
    MLA_Q_RANK,
    MLA_KV_RANK,
    MLA_ROPE_DIM,
    FOX_HEADS * FOX_HEAD_DIM,
    FOX_HEADS * FOX_HEAD_DIM,
    FOX_HEADS * FOX_HEAD_DIM,
    FOX_HEADS,
    D_MODEL,
    D_MODEL,
)
D_IN = int(sum(IN_SPLITS))
SPLIT_IDX = [int(v) for v in np.cumsum(IN_SPLITS)[:-1]]

kernel_name = "hybrid_mla_fox_convffn_gated_merge"


def _rmsnorm(x, g):
    xf = x.astype(jnp.float32)
    y = xf * lax.rsqrt(jnp.mean(xf * xf, axis=-1, keepdims=True) + NORM_EPS)
    return (y * g.astype(jnp.float32)).astype(x.dtype)


def _rope_tables(positions):
    inv_freq = ROPE_THETA ** (-jnp.arange(0, MLA_ROPE_DIM, 2, dtype=jnp.float32) / MLA_ROPE_DIM)
    ang = positions.astype(jnp.float32)[..., None] * inv_freq
    return jnp.cos(ang), jnp.sin(ang)


def _rope(x, cos, sin):
    xf = x.astype(jnp.float32)
    x1, x2 = jnp.split(xf, 2, axis=-1)
    out = jnp.concatenate([x1 * cos - x2 * sin, x2 * cos + x1 * sin], axis=-1)
    return out.astype(x.dtype)


def _sweep_query_blocks(logits_fn, v):
    B, H, S, Dv = v.shape
    n_blocks = S // Q_BLOCK
    key_pos = jnp.arange(S)

    def one_block(i):
        start = i * Q_BLOCK
        logits = logits_fn(start)
        q_pos = start + jnp.arange(Q_BLOCK)
        causal = key_pos[None, :] <= q_pos[:, None]
        logits = jnp.where(causal, logits, -jnp.inf)
        p = jax.nn.softmax(logits, axis=-1)
        return jnp.einsum('bhqk,bhkd->bhqd', p.astype(v.dtype), v)

    out = lax.map(one_block, jnp.arange(n_blocks))
    return out.transpose(1, 2, 0, 3, 4).reshape(B, H, S, Dv)


def _hybrid_mixer(xn, positions, w_in, b_forget, norm_q_a, w_uq, norm_kv_a, w_ukv,
                  w_proj_mla, w_proj_fox, w_out):
    B, S, _ = xn.shape
    proj = xn @ w_in
    c_q, c_kv, k_rope, fq, fk, fv, f_logit, g_mla, g_fox = jnp.split(proj, SPLIT_IDX, axis=-1)

    q = (_rmsnorm(c_q, norm_q_a) @ w_uq).reshape(B, S, MLA_HEADS, MLA_NOPE_DIM + MLA_ROPE_DIM)
    q = q.transpose(0, 2, 1, 3)
    q_nope, q_rope = q[..., :MLA_NOPE_DIM], q[..., MLA_NOPE_DIM:]
    kv = (_rmsnorm(c_kv, norm_kv_a) @ w_ukv).reshape(B, S, MLA_HEADS, MLA_NOPE_DIM + MLA_V_DIM)
    kv = kv.transpose(0, 2, 1, 3)
    k_nope, v_mla = kv[..., :MLA_NOPE_DIM], kv[..., MLA_NOPE_DIM:]
    cos, sin = _rope_tables(positions)
    q_rope = _rope(q_rope, cos[:, None], sin[:, None])
    k_rope = _rope(k_rope, cos, sin)
    mla_scale = (MLA_NOPE_DIM + MLA_ROPE_DIM) ** -0.5

    def mla_logits(start):
        qn = lax.dynamic_slice_in_dim(q_nope, start, Q_BLOCK, axis=2)
        qr = lax.dynamic_slice_in_dim(q_rope, start, Q_BLOCK, axis=2)
        s = (jnp.einsum('bhqd,bhkd->bhqk', qn, k_nope)
             + jnp.einsum('bhqr,bkr->bhqk', qr, k_rope))
        return s.astype(jnp.float32) * mla_scale

    o_mla = _sweep_query_blocks(mla_logits, v_mla)
    o_mla = o_mla.transpose(0, 2, 1, 3).reshape(B, S, MLA_HEADS * MLA_V_DIM)

    def heads(t):
        return t.reshape(B, S, FOX_HEADS, FOX_HEAD_DIM).transpose(0, 2, 1, 3)
    fq, fk, fv = heads(fq), heads(fk), heads(fv)
    log_f = jax.nn.log_sigmoid((f_logit + b_forget).astype(jnp.float32))
    cum = jnp.cumsum(log_f, axis=1).transpose(0, 2, 1)
    fox_scale = FOX_HEAD_DIM ** -0.5

    def fox_logits(start):
        qb = lax.dynamic_slice_in_dim(fq, start, Q_BLOCK, axis=2)
        cq = lax.dynamic_slice_in_dim(cum, start, Q_BLOCK, axis=2)
        s = jnp.einsum('bhqd,bhkd->bhqk', qb, fk).astype(jnp.float32) * fox_scale
        return s + cq[..., :, None] - cum[..., None, :]

    o_fox = _sweep_query_blocks(fox_logits, fv)
    o_fox = o_fox.transpose(0, 2, 1, 3).reshape(B, S, FOX_HEADS * FOX_HEAD_DIM)

    merged = (jax.nn.sigmoid(g_mla) * (o_mla @ w_proj_mla)
              + jax.nn.sigmoid(g_fox) * (o_fox @ w_proj_fox))
    return merged @ w_out


def _causal_dwconv(h, w_conv, b_conv):
    C = h.shape[-1]
    out = lax.conv_general_dilated(
        h, w_conv[:, None, :], window_strides=(1,), padding=((CONV_WIDTH - 1, 0),),
        dimension_numbers=('NWC', 'WIO', 'NWC'), feature_group_count=C)
    return out + b_conv


def _conv_ffn(xn, w_ffn_in, w_conv, b_conv, w_ffn_out):
    h = xn @ w_ffn_in
    h = _causal_dwconv(h, w_conv, b_conv)
    gate, up = jnp.split(h, 2, axis=-1)
    return (jax.nn.silu(gate) * up) @ w_ffn_out


def setup_inputs(seed: int = 0) -> dict:
    key = jax.random.key(seed)
    ks = jax.random.split(key, 20)

    def dense(k, shape, fan_in):
        return jax.random.normal(k, shape, jnp.float32) * fan_in ** -0.5

    def gain(k, n):
        return 1.0 + 0.02 * jax.random.normal(k, (DEPTH, n), jnp.float32)

    x = jax.random.normal(ks[0], (BATCH, SEQ, D_MODEL), jnp.float32)
    offset = jax.random.randint(ks[1], (BATCH, 1), 0, 4096)
    positions = (offset + jnp.arange(SEQ)[None, :]).astype(jnp.int32)
    return {
        "x": x,
        "positions": positions,
        "norm_mix": gain(ks[2], D_MODEL),
        "w_in": dense(ks[3], (DEPTH, D_MODEL, D_IN), D_MODEL),
        "b_forget": 3.0 + 0.5 * jax.random.normal(ks[4], (DEPTH, FOX_HEADS), jnp.float32),
        "norm_q_a": gain(ks[5], MLA_Q_RANK),
        "w_uq": dense(ks[6], (DEPTH, MLA_Q_RANK, MLA_HEADS * (MLA_NOPE_DIM + MLA_ROPE_DIM)), MLA_Q_RANK),
        "norm_kv_a": gain(ks[7], MLA_KV_RANK),
        "w_ukv": dense(ks[8], (DEPTH, MLA_KV_RANK, MLA_HEADS * (MLA_NOPE_DIM + MLA_V_DIM)), MLA_KV_RANK),
        "w_proj_mla": dense(ks[9], (DEPTH, MLA_HEADS * MLA_V_DIM, D_MODEL), MLA_HEADS * MLA_V_DIM),
        "w_proj_fox": dense(ks[10], (DEPTH, FOX_HEADS * FOX_HEAD_DIM, D_MODEL), FOX_HEADS * FOX_HEAD_DIM),
        "w_out": dense(ks[11], (DEPTH, D_MODEL, D_MODEL), D_MODEL),
        "norm_ffn": gain(ks[12], D_MODEL),
        "w_ffn_in": dense(ks[13], (DEPTH, D_MODEL, 2 * D_FF), D_MODEL),
        "w_conv": dense(ks[14], (DEPTH, CONV_WIDTH, 2 * D_FF), CONV_WIDTH),
        "b_conv": 0.02 * jax.random.normal(ks[15], (DEPTH, 2 * D_FF), jnp.float32),
        "w_ffn_out": dense(ks[16], (DEPTH, D_FF, D_MODEL), D_FF),
        "norm_final": 1.0 + 0.02 * jax.random.normal(ks[17], (D_MODEL,), jnp.float32),
    }


def reference(x, positions, norm_mix, w_in, b_forget, norm_q_a, w_uq, norm_kv_a, w_ukv,
              w_proj_mla, w_proj_fox, w_out, norm_ffn, w_ffn_in, w_conv, b_conv, w_ffn_out,
              norm_final):
    h = x
    for l in range(DEPTH):
        xn = _rmsnorm(h, norm_mix[l])
        h = h + _hybrid_mixer(xn, positions, w_in[l], b_forget[l], norm_q_a[l], w_uq[l],
                              norm_kv_a[l], w_ukv[l], w_proj_mla[l], w_proj_fox[l], w_out[l])
        xn = _rmsnorm(h, norm_ffn[l])
        h = h + _conv_ffn(xn, w_ffn_in[l], w_conv[l], b_conv[l], w_ffn_out[l])
    return _rmsnorm(h, norm_final)
```

```python
import functools
import math

import jax
import jax.numpy as jnp
import numpy as np
from jax import lax
from jax.experimental import pallas as pl
from jax.experimental.pallas import tpu as pltpu

F32 = jnp.float32
BF16 = jnp.bfloat16

D_MODEL = 2048
MLA_HEADS = 8
MLA_Q_RANK = 512
MLA_KV_RANK = 512
MLA_NOPE_DIM = 128
MLA_ROPE_DIM = 64
MLA_V_DIM = 128
FOX_HEADS = 8
FOX_HEAD_DIM = 128
D_FF = 5632
CONV_WIDTH = 3
ROPE_THETA = 10000.0
NORM_EPS = 1e-6

LANES = 128
QK_WIDTH = 2 * LANES
ROPE_HALF = MLA_ROPE_DIM // 2
LOG2E = math.log2(math.e)
VMEM_LIMIT_BYTES = 56 * 1024 * 1024


def _params(n_grid_dims):
    return pltpu.CompilerParams(
        dimension_semantics=("arbitrary",) * n_grid_dims,
        vmem_limit_bytes=VMEM_LIMIT_BYTES)


def _rms(x, g):
    y = x * lax.rsqrt(jnp.mean(x * x, axis=-1, keepdims=True) + NORM_EPS)
    return y * g


def _rmsnorm_kernel(x_ref, g_ref, o_ref):
    o_ref[...] = _rms(x_ref[...], g_ref[...]).astype(o_ref.dtype)


def _rmsnorm(x, g, tm):
    s, d = x.shape
    return pl.pallas_call(
        _rmsnorm_kernel,
        grid=(s // tm,),
        in_specs=[pl.BlockSpec((tm, d), lambda i: (i, 0)),
                  pl.BlockSpec((1, d), lambda i: (0, 0))],
        out_specs=pl.BlockSpec((tm, d), lambda i: (i, 0)),
        out_shape=jax.ShapeDtypeStruct((s, d), BF16),
        compiler_params=_params(1),
        name="rmsnorm",
    )(x, g)


def _mm_kernel(a_ref, w_ref, *rest, epilogue):
    o_ref = rest[-1]
    acc = jnp.dot(a_ref[...], w_ref[...], preferred_element_type=F32)
    if epilogue is not None:
        acc = epilogue(acc, *[r[...] for r in rest[:-1]])
    o_ref[...] = acc.astype(o_ref.dtype)


def _matmul(a, w, out_dtype, tm, tn, epilogue=None, extras=(), name="matmul"):
    m, k = a.shape
    n = w.shape[1]
    in_specs = [pl.BlockSpec((tm, k), lambda i, j: (i, 0)),
                pl.BlockSpec((k, tn), lambda i, j: (0, j))]
    in_specs += [pl.BlockSpec((tm, tn), lambda i, j: (i, j)) for _ in extras]
    return pl.pallas_call(
        functools.partial(_mm_kernel, epilogue=epilogue),
        grid=(m // tm, n // tn),
        in_specs=in_specs,
        out_specs=pl.BlockSpec((tm, tn), lambda i, j: (i, j)),
        out_shape=jax.ShapeDtypeStruct((m, n), out_dtype),
        compiler_params=_params(2),
        name=name,
    )(a, w, *extras)


def _rope_table_kernel(pos_ref, invf_ref, cos_ref, sin_ref, nsin_ref):
    ang = pos_ref[...].astype(F32) * invf_ref[...]
    s = jnp.sin(ang)
    cos_ref[...] = jnp.cos(ang)
    sin_ref[...] = s
    nsin_ref[...] = -s


def _rope_tables(pos_dense, invf_dense, tb):
    r = pos_dense.shape[0]
    spec = pl.BlockSpec((tb, LANES), lambda i: (i, 0))
    return pl.pallas_call(
        _rope_table_kernel,
        grid=(r // tb,),
        in_specs=[spec, pl.BlockSpec((1, LANES), lambda i: (0, 0))],
        out_specs=[spec, spec, spec],
        out_shape=[jax.ShapeDtypeStruct((r, LANES), F32)] * 3,
        compiler_params=_params(1),
        name="rope_tables",
    )(pos_dense, invf_dense)


def _split3(x):
    p1 = x.astype(BF16)
    r1 = x - p1.astype(F32)
    p2 = r1.astype(BF16)
    p3 = (r1 - p2.astype(F32)).astype(BF16)
    return p1, p2, p3


def _decay_kernel(fl_ref, b_ref, o_ref, carry_ref, *, tb):
    @pl.when(pl.program_id(0) == 0)
    def _():
        carry_ref[...] = jnp.zeros_like(carry_ref)

    z = fl_ref[...] + b_ref[...]
    logf = jnp.minimum(z, 0.0) - jnp.log1p(jnp.exp(-jnp.abs(z)))
    row = lax.broadcasted_iota(jnp.int32, (tb, tb), 0)
    col = lax.broadcasted_iota(jnp.int32, (tb, tb), 1)
    tri = (col <= row).astype(BF16)
    c = carry_ref[...]
    for piece in _split3(logf):
        c = c + jnp.dot(tri, piece, preferred_element_type=F32)
    o_ref[...] = c * LOG2E
    carry_ref[...] = c[tb - 1:tb, :]


def _decay_cumsum(small, fl_col, b_tile, tb):
    s = small.shape[0]
    return pl.pallas_call(
        functools.partial(_decay_kernel, tb=tb),
        grid=(s // tb,),
        in_specs=[pl.BlockSpec((tb, LANES), lambda i: (i, fl_col)),
                  pl.BlockSpec((1, LANES), lambda i: (0, 0))],
        out_specs=pl.BlockSpec((tb, LANES), lambda i: (i, 0)),
        out_shape=jax.ShapeDtypeStruct((s, LANES), F32),
        scratch_shapes=[pltpu.VMEM((1, LANES), F32)],
        compiler_params=_params(1),
        name="decay_cumsum",
    )(small, b_tile)


def _mla_prep_kernel(cq_ref, ckv_ref, ka_ref, kb_ref, ct_ref, st_ref, gq_ref, gkv_ref,
                     wqn_ref, wqa_ref, wqb_ref, wkn_ref, wv_ref,
                     q_ref, k_ref, v_ref, *, q_scale):
    ct = ct_ref[...]
    st = st_ref[...]
    cq = _rms(cq_ref[...], gq_ref[...]).astype(BF16)
    ckv = _rms(ckv_ref[...], gkv_ref[...]).astype(BF16)
    qn = jnp.dot(cq, wqn_ref[...], preferred_element_type=F32)
    qa = jnp.dot(cq, wqa_ref[...], preferred_element_type=F32)
    qb = jnp.dot(cq, wqb_ref[...], preferred_element_type=F32)
    kn = jnp.dot(ckv, wkn_ref[...], preferred_element_type=F32)
    v_ref[...] = jnp.dot(ckv, wv_ref[...], preferred_element_type=F32).astype(v_ref.dtype)
    kr = (ka_ref[...] * ct + kb_ref[...] * st).astype(k_ref.dtype)
    for h in range(MLA_HEADS):
        lo, hi = h * LANES, (h + 1) * LANES
        base = h * QK_WIDTH
        q_ref[:, base:base + LANES] = (qn[:, lo:hi] * q_scale).astype(q_ref.dtype)
        q_ref[:, base + LANES:base + QK_WIDTH] = (
            (qa[:, lo:hi] * ct + qb[:, lo:hi] * st) * q_scale).astype(q_ref.dtype)
        k_ref[:, base:base + LANES] = kn[:, lo:hi].astype(k_ref.dtype)
        k_ref[:, base + LANES:base + QK_WIDTH] = kr


def _mla_prep(small, ct, st, gq, gkv, wqn, wqa, wqb, wkn, wv, tm):
    s = small.shape[0]
    hw = MLA_HEADS * LANES
    full = lambda shape: pl.BlockSpec(shape, lambda i: (0, 0))
    rows = lambda width, col: pl.BlockSpec((tm, width), lambda i, c=col: (i, c))
    q_scale = (MLA_NOPE_DIM + MLA_ROPE_DIM) ** -0.5 * LOG2E
    return pl.pallas_call(
        functools.partial(_mla_prep_kernel, q_scale=q_scale),
        grid=(s // tm,),
        in_specs=[rows(MLA_Q_RANK, 0), rows(MLA_KV_RANK, 1),
                  rows(LANES, (MLA_Q_RANK + MLA_KV_RANK) // LANES),
                  rows(LANES, (MLA_Q_RANK + MLA_KV_RANK) // LANES + 1),
                  rows(LANES, 0), rows(LANES, 0),
                  full((1, MLA_Q_RANK)), full((1, MLA_KV_RANK)),
                  full((MLA_Q_RANK, hw)), full((MLA_Q_RANK, hw)), full((MLA_Q_RANK, hw)),
                  full((MLA_KV_RANK, hw)), full((MLA_KV_RANK, hw))],
        out_specs=[pl.BlockSpec((tm, MLA_HEADS * QK_WIDTH), lambda i: (i, 0)),
                   pl.BlockSpec((tm, MLA_HEADS * QK_WIDTH), lambda i: (i, 0)),
                   pl.BlockSpec((tm, hw), lambda i: (i, 0))],
        out_shape=[jax.ShapeDtypeStruct((s, MLA_HEADS * QK_WIDTH), BF16),
                   jax.ShapeDtypeStruct((s, MLA_HEADS * QK_WIDTH), BF16),
                   jax.ShapeDtypeStruct((s, hw), BF16)],
        compiler_params=_params(1),
        name="mla_prep",
    )(small, small, small, small, ct, st, gq, gkv, wqn, wqa, wqb, wkn, wv)


def _fox_prep_kernel(xn_ref, cum_ref, wq_ref, wk_ref, pq_ref, pk_ref, oq_ref, ok_ref,
                     q_ref, k_ref, *, q_scale):
    xn = xn_ref[...]
    fq = jnp.dot(xn, wq_ref[...], preferred_element_type=F32) * q_scale
    fk = jnp.dot(xn, wk_ref[...], preferred_element_type=F32)
    qb = oq_ref[...]
    kb = ok_ref[...]
    for idx, piece in enumerate(_split3(cum_ref[...])):
        qb = qb + jnp.dot(piece, pq_ref[idx], preferred_element_type=F32)
        kb = kb + jnp.dot(piece, pk_ref[idx], preferred_element_type=F32)
    for h in range(FOX_HEADS):
        lo, hi = h * LANES, (h + 1) * LANES
        base = h * QK_WIDTH
        q_ref[:, base:base + LANES] = fq[:, lo:hi].astype(q_ref.dtype)
        q_ref[:, base + LANES:base + QK_WIDTH] = qb[:, lo:hi].astype(q_ref.dtype)
        k_ref[:, base:base + LANES] = fk[:, lo:hi].astype(k_ref.dtype)
        k_ref[:, base + LANES:base + QK_WIDTH] = kb[:, lo:hi].astype(k_ref.dtype)


def _fox_bias_constants():
    hw = FOX_HEADS * LANES
    pq = np.zeros((3, LANES, hw), np.float32)
    pk = np.zeros((3, LANES, hw), np.float32)
    oq = np.zeros((1, hw), np.float32)
    ok = np.zeros((1, hw), np.float32)
    for h in range(FOX_HEADS):
        for p in range(3):
            pq[p, h, h * LANES + p] = 1.0
            oq[0, h * LANES + 3 + p] = 1.0
            pk[p, h, h * LANES + 3 + p] = -1.0
            ok[0, h * LANES + p] = 1.0
    return (jnp.asarray(pq, BF16), jnp.asarray(pk, BF16), jnp.asarray(oq), jnp.asarray(ok))


def _fox_prep(xn, cum, wq, wk, tm):
    s, d = xn.shape
    hw = FOX_HEADS * LANES
    pq, pk, oq, ok = _fox_bias_constants()
    q_scale = FOX_HEAD_DIM ** -0.5 * LOG2E
    return pl.pallas_call(
        functools.partial(_fox_prep_kernel, q_scale=q_scale),
        grid=(s // tm,),
        in_specs=[pl.BlockSpec((tm, d), lambda i: (i, 0)),
                  pl.BlockSpec((tm, LANES), lambda i: (i, 0)),
                  pl.BlockSpec((d, hw), lambda i: (0, 0)),
                  pl.BlockSpec((d, hw), lambda i: (0, 0)),
                  pl.BlockSpec((3, LANES, hw), lambda i: (0, 0, 0)),
                  pl.BlockSpec((3, LANES, hw), lambda i: (0, 0, 0)),
                  pl.BlockSpec((1, hw), lambda i: (0, 0)),
                  pl.BlockSpec((1, hw), lambda i: (0, 0))],
        out_specs=[pl.BlockSpec((tm, FOX_HEADS * QK_WIDTH), lambda i: (i, 0))] * 2,
        out_shape=[jax.ShapeDtypeStruct((s, FOX_HEADS * QK_WIDTH), BF16)] * 2,
        compiler_params=_params(1),
        name="fox_prep",
    )(xn, cum, wq, wk, pq, pk, oq, ok)


def _attn_kernel(q_ref, k_ref, v_ref, o_ref, m_ref, l_ref, acc_ref, *, tq, tk):
    i = pl.program_id(1)
    q = q_ref[...]
    m_ref[...] = jnp.full_like(m_ref, -jnp.inf)
    l_ref[...] = jnp.zeros_like(l_ref)
    acc_ref[...] = jnp.zeros_like(acc_ref)
    ratio = tq // tk

    def step(j, masked):
        start = pl.multiple_of(j * tk, tk)
        k = k_ref[pl.ds(start, tk), :]
        v = v_ref[pl.ds(start, tk), :]
        s = lax.dot_general(q, k, (((1,), (1,)), ((), ())), preferred_element_type=F32)
        if masked:
            row = lax.broadcasted_iota(jnp.int32, (tq, tk), 0)
            col = lax.broadcasted_iota(jnp.int32, (tq, tk), 1)
            s = jnp.where(col + (j * tk - i * tq) <= row, s, -jnp.inf)
        m_prev = m_ref[...]
        m_new = jnp.maximum(m_prev, jnp.max(s, axis=1, keepdims=True))
        alpha = jnp.exp2(m_prev - m_new)
        p = jnp.exp2(s - m_new)
        l_ref[...] = alpha * l_ref[...] + jnp.sum(p, axis=1, keepdims=True)
        acc_ref[...] = alpha * acc_ref[...] + jnp.dot(
            p.astype(BF16), v, preferred_element_type=F32)
        m_ref[...] = m_new

    def full_step(j, carry):
        step(j, masked=False)
        return carry

    lax.fori_loop(0, i * ratio, full_step, 0)
    for r in range(ratio):
        step(i * ratio + r, masked=True)
    o_ref[...] = (acc_ref[...] / l_ref[...]).astype(o_ref.dtype)


def _attention(q, k, v, n_heads, dv, tq, tk, name):
    s = q.shape[0]
    return pl.pallas_call(
        functools.partial(_attn_kernel, tq=tq, tk=tk),
        grid=(n_heads, s // tq),
        in_specs=[pl.BlockSpec((tq, QK_WIDTH), lambda h, i: (i, h)),
                  pl.BlockSpec((s, QK_WIDTH), lambda h, i: (0, h)),
                  pl.BlockSpec((s, dv), lambda h, i: (0, h))],
        out_specs=pl.BlockSpec((tq, dv), lambda h, i: (i, h)),
        out_shape=jax.ShapeDtypeStruct((s, n_heads * dv), BF16),
        scratch_shapes=[pltpu.VMEM((tq, 1), F32), pltpu.VMEM((tq, 1), F32),
                        pltpu.VMEM((tq, dv), F32)],
        compiler_params=_params(2),
        name=name,
    )(q, k, v)


def _merge_kernel(om_ref, of_ref, wm_ref, wf_ref, gm_ref, gf_ref, o_ref):
    pm = jnp.dot(om_ref[...], wm_ref[...], preferred_element_type=F32)
    pf = jnp.dot(of_ref[...], wf_ref[...], preferred_element_type=F32)
    o_ref[...] = (gm_ref[...].astype(F32) * pm + gf_ref[...].astype(F32) * pf).astype(o_ref.dtype)


def _merge(o_mla, o_fox, wm, wf, gates, tm, tn):
    s, kd = o_mla.shape
    n = wm.shape[1]
    nb = n // tn
    return pl.pallas_call(
        _merge_kernel,
        grid=(s // tm, nb),
        in_specs=[pl.BlockSpec((tm, kd), lambda i, j: (i, 0)),
                  pl.BlockSpec((tm, kd), lambda i, j: (i, 0)),
                  pl.BlockSpec((kd, tn), lambda i, j: (0, j)),
                  pl.BlockSpec((kd, tn), lambda i, j: (0, j)),
                  pl.BlockSpec((tm, tn), lambda i, j: (i, j)),
                  pl.BlockSpec((tm, tn), lambda i, j: (i, j + nb))],
        out_specs=pl.BlockSpec((tm, tn), lambda i, j: (i, j)),
        out_shape=jax.ShapeDtypeStruct((s, n), BF16),
        compiler_params=_params(2),
        name="gated_merge",
    )(o_mla, o_fox, wm, wf, gates, gates)


HALO = 16


def _ffn_kernel(h_ref, halo_ref, g_ref, wg_ref, wu_ref, cg_ref, cu_ref, bg_ref, bu_ref,
                wo_ref, gf_ref, y_ref, xn_ref, acc_ref, *, tm):
    i = pl.program_id(0)
    j = pl.program_id(1)

    @pl.when(j == 0)
    def _():
        h = h_ref[...]
        g = g_ref[...]
        keep = jnp.where(i > 0, 1.0, 0.0)
        xn_ref[0:HALO, :] = (_rms(halo_ref[...], g) * keep).astype(xn_ref.dtype)
        xn_ref[HALO:HALO + tm, :] = _rms(h, g).astype(xn_ref.dtype)
        acc_ref[...] = h

    xn = xn_ref[...]

    def conv(w_ref, c_ref, b_ref):
        hh = jnp.dot(xn, w_ref[...], preferred_element_type=F32)
        c = c_ref[...]
        return (hh[HALO:HALO + tm] * c[2:3] + hh[HALO - 1:HALO - 1 + tm] * c[1:2]
                + hh[HALO - 2:HALO - 2 + tm] * c[0:1] + b_ref[...])

    gate = conv(wg_ref, cg_ref, bg_ref)
    up = conv(wu_ref, cu_ref, bu_ref)
    act = (gate * jax.nn.sigmoid(gate) * up).astype(BF16)
    acc_ref[...] += jnp.dot(act, wo_ref[...], preferred_element_type=F32)

    @pl.when(j == pl.num_programs(1) - 1)
    def _():
        y_ref[...] = _rms(acc_ref[...], gf_ref[...]).astype(y_ref.dtype)


def _conv_ffn(h1, g_ffn, w_in, w_conv, b_conv, w_out, g_final, tm, tn):
    s, d = h1.shape
    f = w_out.shape[0]
    nb = f // tn
    hb = tm // HALO
    return pl.pallas_call(
        functools.partial(_ffn_kernel, tm=tm),
        grid=(s // tm, nb),
        in_specs=[pl.BlockSpec((tm, d), lambda i, j: (i, 0)),
                  pl.BlockSpec((HALO, d), lambda i, j: (jnp.maximum(i * hb - 1, 0), 0)),
                  pl.BlockSpec((1, d), lambda i, j: (0, 0)),
                  pl.BlockSpec((d, tn), lambda i, j: (0, j)),
                  pl.BlockSpec((d, tn), lambda i, j: (0, j + nb)),
                  pl.BlockSpec((CONV_WIDTH, tn), lambda i, j: (0, j)),
                  pl.BlockSpec((CONV_WIDTH, tn), lambda i, j: (0, j + nb)),
                  pl.BlockSpec((1, tn), lambda i, j: (0, j)),
                  pl.BlockSpec((1, tn), lambda i, j: (0, j + nb)),
                  pl.BlockSpec((tn, d), lambda i, j: (j, 0)),
                  pl.BlockSpec((1, d), lambda i, j: (0, 0))],
        out_specs=pl.BlockSpec((tm, d), lambda i, j: (i, 0)),
        out_shape=jax.ShapeDtypeStruct((s, d), F32),
        scratch_shapes=[pltpu.VMEM((HALO + tm, d), BF16), pltpu.VMEM((tm, d), F32)],
        compiler_params=_params(2),
        name="conv_ffn",
    )(h1, h1, g_ffn, w_in, w_in, w_conv, w_conv, b_conv, b_conv, w_out, g_final)


def _pad_cols(w, width):
    return jnp.pad(w, ((0, 0), (0, width - w.shape[1])))


def _layer(h, positions, norm_mix, w_in, b_forget, norm_q_a, w_uq, norm_kv_a, w_ukv,
           w_proj_mla, w_proj_fox, w_out, norm_ffn, w_ffn_in, w_conv, b_conv, w_ffn_out,
           final_gain):
    s = h.shape[0]
    tm = min(512, s)
    hw = MLA_HEADS * LANES

    o = np.cumsum([0, MLA_Q_RANK, MLA_KV_RANK, MLA_ROPE_DIM, hw, hw, hw, FOX_HEADS,
                   D_MODEL, D_MODEL])
    w_lat = w_in[:, o[0]:o[2]]
    w_kr = w_in[:, o[2]:o[3]]
    w_kr_swapped = jnp.concatenate([w_kr[:, ROPE_HALF:], w_kr[:, :ROPE_HALF]], axis=1)
    w_small = jnp.concatenate(
        [w_lat, _pad_cols(w_kr, LANES), _pad_cols(w_kr_swapped, LANES),
         _pad_cols(w_in[:, o[6]:o[7]], LANES)], axis=1).astype(BF16)
    w_fq = w_in[:, o[3]:o[4]].astype(BF16)
    w_fk = w_in[:, o[4]:o[5]].astype(BF16)
    w_fv = w_in[:, o[5]:o[6]].astype(BF16)
    w_gates = w_in[:, o[7]:o[9]].astype(BF16)

    uq = w_uq.reshape(MLA_Q_RANK, MLA_HEADS, MLA_NOPE_DIM + MLA_ROPE_DIM)
    w_qn = uq[:, :, :MLA_NOPE_DIM].reshape(MLA_Q_RANK, hw).astype(BF16)
    r1 = uq[:, :, MLA_NOPE_DIM:MLA_NOPE_DIM + ROPE_HALF]
    r2 = uq[:, :, MLA_NOPE_DIM + ROPE_HALF:]
    zpad = jnp.zeros((MLA_Q_RANK, MLA_HEADS, LANES - MLA_ROPE_DIM), w_uq.dtype)
    w_qa = jnp.concatenate([r1, r2, zpad], axis=2).reshape(MLA_Q_RANK, hw).astype(BF16)
    w_qb = jnp.concatenate([r2, r1, zpad], axis=2).reshape(MLA_Q_RANK, hw).astype(BF16)
    ukv = w_ukv.reshape(MLA_KV_RANK, MLA_HEADS, MLA_NOPE_DIM + MLA_V_DIM)
    w_kn = ukv[:, :, :MLA_NOPE_DIM].reshape(MLA_KV_RANK, hw).astype(BF16)
    w_v = ukv[:, :, MLA_NOPE_DIM:].reshape(MLA_KV_RANK, hw).astype(BF16)

    xn = _rmsnorm(h, norm_mix[None, :], tm)
    small = _matmul(xn, w_small, F32, tm, w_small.shape[1], name="proj_latents")
    fv = _matmul(xn, w_fv, BF16, tm, hw, name="proj_fox_v")
    gates = _matmul(xn, w_gates, BF16, tm, 1024, epilogue=jax.nn.sigmoid, name="proj_gates")

    per_row = LANES // ROPE_HALF
    inv_freq = ROPE_THETA ** (-jnp.arange(0, MLA_ROPE_DIM, 2, dtype=F32) / MLA_ROPE_DIM)
    pos_dense = jnp.repeat(positions, ROPE_HALF).reshape(s // per_row, LANES)
    invf_dense = jnp.tile(inv_freq, per_row)[None, :]
    cos_d, sin_d, nsin_d = _rope_tables(pos_dense, invf_dense, min(512, s // per_row))
    cos, sin, nsin = (t.reshape(s, ROPE_HALF) for t in (cos_d, sin_d, nsin_d))
    zeros = jnp.zeros((s, LANES - MLA_ROPE_DIM), F32)
    ct = jnp.concatenate([cos, cos, zeros], axis=1)
    st = jnp.concatenate([nsin, sin, zeros], axis=1)

    q_mla, k_mla, v_mla = _mla_prep(small, ct, st, norm_q_a[None, :], norm_kv_a[None, :],
                                    w_qn, w_qa, w_qb, w_kn, w_v, tm)

    fl_col = (MLA_Q_RANK + MLA_KV_RANK) // LANES + 2
    cum = _decay_cumsum(small, fl_col, _pad_cols(b_forget[None, :], LANES), min(256, s))
    q_fox, k_fox = _fox_prep(xn, cum, w_fq, w_fk, tm)

    tq = min(512, s)
    o_mla = _attention(q_mla, k_mla, v_mla, MLA_HEADS, MLA_V_DIM, tq, tq, "attn_mla")
    o_fox = _attention(q_fox, k_fox, fv, FOX_HEADS, FOX_HEAD_DIM, tq, tq, "attn_fox")

    merged = _merge(o_mla, o_fox, w_proj_mla.astype(BF16), w_proj_fox.astype(BF16), gates,
                    tm, 1024)
    h1 = _matmul(merged, w_out.astype(BF16), F32, tm, 1024,
                 epilogue=lambda acc, res: acc + res, extras=(h,), name="proj_out")

    return _conv_ffn(h1, norm_ffn[None, :], w_ffn_in.astype(BF16), w_conv, b_conv[None, :],
                     w_ffn_out.astype(BF16), final_gain[None, :], tm, 512)


def kernel(x, positions, norm_mix, w_in, b_forget, norm_q_a, w_uq, norm_kv_a, w_ukv, w_proj_mla,
           w_proj_fox, w_out, norm_ffn, w_ffn_in, w_conv, b_conv, w_ffn_out, norm_final):
    batch, _, _ = x.shape
    depth = w_in.shape[0]
    assert depth == 1, "the fused ConvFFN folds the final norm into the only layer"
    outs = []
    for b in range(batch):
        y = _layer(x[b], positions[b], norm_mix[0], w_in[0], b_forget[0], norm_q_a[0], w_uq[0],
                   norm_kv_a[0], w_ukv[0], w_proj_mla[0], w_proj_fox[0], w_out[0], norm_ffn[0],
                   w_ffn_in[0], w_conv[0], b_conv[0], w_ffn_out[0], norm_final)
        outs.append(y)
    return jnp.stack(outs, axis=0)
```

```python
import functools
import math

import jax
import jax.numpy as jnp
import numpy as np
from jax import lax
from jax.experimental import pallas as pl
from jax.experimental.pallas import tpu as pltpu

F32 = jnp.float32
BF16 = jnp.bfloat16

D_MODEL = 2048
MLA_HEADS = 8
MLA_Q_RANK = 512
MLA_KV_RANK = 512
MLA_NOPE_DIM = 128
MLA_ROPE_DIM = 64
MLA_V_DIM = 128
FOX_HEADS = 8
FOX_HEAD_DIM = 128
D_FF = 5632
CONV_WIDTH = 3
ROPE_THETA = 10000.0
NORM_EPS = 1e-6

LANES = 128
QK_WIDTH = 2 * LANES
ROPE_HALF = MLA_ROPE_DIM // 2
LOG2E = math.log2(math.e)
VMEM_LIMIT_BYTES = 56 * 1024 * 1024


def _params(n_grid_dims):
    return pltpu.CompilerParams(
        dimension_semantics=("arbitrary",) * n_grid_dims,
        vmem_limit_bytes=VMEM_LIMIT_BYTES)


def _rms(x, g):
    y = x * lax.rsqrt(jnp.mean(x * x, axis=-1, keepdims=True) + NORM_EPS)
    return y * g


def _rmsnorm_kernel(x_ref, g_ref, o_ref):
    o_ref[...] = _rms(x_ref[...], g_ref[...]).astype(o_ref.dtype)


def _rmsnorm(x, g, tm):
    s, d = x.shape
    return pl.pallas_call(
        _rmsnorm_kernel,
        grid=(s // tm,),
        in_specs=[pl.BlockSpec((tm, d), lambda i: (i, 0)),
                  pl.BlockSpec((1, d), lambda i: (0, 0))],
        out_specs=pl.BlockSpec((tm, d), lambda i: (i, 0)),
        out_shape=jax.ShapeDtypeStruct((s, d), BF16),
        compiler_params=_params(1),
        name="rmsnorm",
    )(x, g)


def _mm_kernel(a_ref, w_ref, *rest, epilogue):
    o_ref = rest[-1]
    acc = jnp.dot(a_ref[...], w_ref[...], preferred_element_type=F32)
    if epilogue is not None:
        acc = epilogue(acc, *[r[...] for r in rest[:-1]])
    o_ref[...] = acc.astype(o_ref.dtype)


def _matmul(a, w, out_dtype, tm, tn, epilogue=None, extras=(), name="matmul"):
    m, k = a.shape
    n = w.shape[1]
    in_specs = [pl.BlockSpec((tm, k), lambda i, j: (i, 0)),
                pl.BlockSpec((k, tn), lambda i, j: (0, j))]
    in_specs += [pl.BlockSpec((tm, tn), lambda i, j: (i, j)) for _ in extras]
    return pl.pallas_call(
        functools.partial(_mm_kernel, epilogue=epilogue),
        grid=(m // tm, n // tn),
        in_specs=in_specs,
        out_specs=pl.BlockSpec((tm, tn), lambda i, j: (i, j)),
        out_shape=jax.ShapeDtypeStruct((m, n), out_dtype),
        compiler_params=_params(2),
        name=name,
    )(a, w, *extras)


def _mm_t_kernel(wt_ref, a_ref, o_ref):
    o_ref[...] = lax.dot_general(wt_ref[...], a_ref[...], (((1,), (1,)), ((), ())),
                                 preferred_element_type=F32).astype(o_ref.dtype)


def _matmul_t(wt, a, out_dtype, tm, name):
    n, k = wt.shape
    m = a.shape[0]
    return pl.pallas_call(
        _mm_t_kernel,
        grid=(m // tm,),
        in_specs=[pl.BlockSpec((n, k), lambda i: (0, 0)),
                  pl.BlockSpec((tm, k), lambda i: (i, 0))],
        out_specs=pl.BlockSpec((n, tm), lambda i: (0, i)),
        out_shape=jax.ShapeDtypeStruct((n, m), out_dtype),
        compiler_params=_params(1),
        name=name,
    )(wt, a)


def _rope_table_kernel(pos_ref, invf_ref, cos_ref, sin_ref, nsin_ref):
    ang = pos_ref[...].astype(F32) * invf_ref[...]
    s = jnp.sin(ang)
    cos_ref[...] = jnp.cos(ang)
    sin_ref[...] = s
    nsin_ref[...] = -s


def _rope_tables(pos_dense, invf_dense, tb):
    r = pos_dense.shape[0]
    spec = pl.BlockSpec((tb, LANES), lambda i: (i, 0))
    return pl.pallas_call(
        _rope_table_kernel,
        grid=(r // tb,),
        in_specs=[spec, pl.BlockSpec((1, LANES), lambda i: (0, 0))],
        out_specs=[spec, spec, spec],
        out_shape=[jax.ShapeDtypeStruct((r, LANES), F32)] * 3,
        compiler_params=_params(1),
        name="rope_tables",
    )(pos_dense, invf_dense)


def _split3(x):
    p1 = x.astype(BF16)
    r1 = x - p1.astype(F32)
    p2 = r1.astype(BF16)
    p3 = (r1 - p2.astype(F32)).astype(BF16)
    return p1, p2, p3


def _decay_kernel(fl_ref, b_ref, o_ref, carry_ref, *, tb):
    @pl.when(pl.program_id(0) == 0)
    def _():
        carry_ref[...] = jnp.zeros_like(carry_ref)

    z = fl_ref[...] + b_ref[...]
    logf = jnp.minimum(z, 0.0) - jnp.log1p(jnp.exp(-jnp.abs(z)))
    row = lax.broadcasted_iota(jnp.int32, (tb, tb), 0)
    col = lax.broadcasted_iota(jnp.int32, (tb, tb), 1)
    tri = (col <= row).astype(BF16)
    c = carry_ref[...]
    for piece in _split3(logf):
        c = c + jnp.dot(tri, piece, preferred_element_type=F32)
    o_ref[...] = c * LOG2E
    carry_ref[...] = c[tb - 1:tb, :]


def _decay_cumsum(small, fl_col, b_tile, tb):
    s = small.shape[0]
    return pl.pallas_call(
        functools.partial(_decay_kernel, tb=tb),
        grid=(s // tb,),
        in_specs=[pl.BlockSpec((tb, LANES), lambda i: (i, fl_col)),
                  pl.BlockSpec((1, LANES), lambda i: (0, 0))],
        out_specs=pl.BlockSpec((tb, LANES), lambda i: (i, 0)),
        out_shape=jax.ShapeDtypeStruct((s, LANES), F32),
        scratch_shapes=[pltpu.VMEM((1, LANES), F32)],
        compiler_params=_params(1),
        name="decay_cumsum",
    )(small, b_tile)


def _mla_prep_kernel(cq_ref, ckv_ref, ka_ref, kb_ref, ct_ref, st_ref, gq_ref, gkv_ref,
                     wqn_ref, wqa_ref, wqb_ref, wkn_ref, wvt_ref,
                     q_ref, k_ref, vt_ref, *, q_scale):
    ct = ct_ref[...]
    st = st_ref[...]
    cq = _rms(cq_ref[...], gq_ref[...]).astype(BF16)
    ckv = _rms(ckv_ref[...], gkv_ref[...]).astype(BF16)
    qn = jnp.dot(cq, wqn_ref[...], preferred_element_type=F32)
    qa = jnp.dot(cq, wqa_ref[...], preferred_element_type=F32)
    qb = jnp.dot(cq, wqb_ref[...], preferred_element_type=F32)
    kn = jnp.dot(ckv, wkn_ref[...], preferred_element_type=F32)
    vt_ref[...] = lax.dot_general(wvt_ref[...], ckv, (((1,), (1,)), ((), ())),
                                  preferred_element_type=F32).astype(vt_ref.dtype)
    kr = (ka_ref[...] * ct + kb_ref[...] * st).astype(k_ref.dtype)
    for h in range(MLA_HEADS):
        lo, hi = h * LANES, (h + 1) * LANES
        base = h * QK_WIDTH
        q_ref[:, base:base + LANES] = (qn[:, lo:hi] * q_scale).astype(q_ref.dtype)
        q_ref[:, base + LANES:base + QK_WIDTH] = (
            (qa[:, lo:hi] * ct + qb[:, lo:hi] * st) * q_scale).astype(q_ref.dtype)
        k_ref[:, base:base + LANES] = kn[:, lo:hi].astype(k_ref.dtype)
        k_ref[:, base + LANES:base + QK_WIDTH] = kr


def _mla_prep(small, ct, st, gq, gkv, wqn, wqa, wqb, wkn, wvt, tm):
    s = small.shape[0]
    hw = MLA_HEADS * LANES
    full = lambda shape: pl.BlockSpec(shape, lambda i: (0, 0))
    rows = lambda width, col: pl.BlockSpec((tm, width), lambda i, c=col: (i, c))
    q_scale = (MLA_NOPE_DIM + MLA_ROPE_DIM) ** -0.5 * LOG2E
    return pl.pallas_call(
        functools.partial(_mla_prep_kernel, q_scale=q_scale),
        grid=(s // tm,),
        in_specs=[rows(MLA_Q_RANK, 0), rows(MLA_KV_RANK, 1),
                  rows(LANES, (MLA_Q_RANK + MLA_KV_RANK) // LANES),
                  rows(LANES, (MLA_Q_RANK + MLA_KV_RANK) // LANES + 1),
                  rows(LANES, 0), rows(LANES, 0),
                  full((1, MLA_Q_RANK)), full((1, MLA_KV_RANK)),
                  full((MLA_Q_RANK, hw)), full((MLA_Q_RANK, hw)), full((MLA_Q_RANK, hw)),
                  full((MLA_KV_RANK, hw)), full((hw, MLA_KV_RANK))],
        out_specs=[pl.BlockSpec((tm, MLA_HEADS * QK_WIDTH), lambda i: (i, 0)),
                   pl.BlockSpec((tm, MLA_HEADS * QK_WIDTH), lambda i: (i, 0)),
                   pl.BlockSpec((hw, tm), lambda i: (0, i))],
        out_shape=[jax.ShapeDtypeStruct((s, MLA_HEADS * QK_WIDTH), BF16),
                   jax.ShapeDtypeStruct((s, MLA_HEADS * QK_WIDTH), BF16),
                   jax.ShapeDtypeStruct((hw, s), BF16)],
        compiler_params=_params(1),
        name="mla_prep",
    )(small, small, small, small, ct, st, gq, gkv, wqn, wqa, wqb, wkn, wvt)


def _fox_prep_kernel(xn_ref, cum_ref, wq_ref, wk_ref, pq_ref, pk_ref, oq_ref, ok_ref,
                     q_ref, k_ref, *, q_scale):
    xn = xn_ref[...]
    fq = jnp.dot(xn, wq_ref[...], preferred_element_type=F32) * q_scale
    fk = jnp.dot(xn, wk_ref[...], preferred_element_type=F32)
    qb = oq_ref[...]
    kb = ok_ref[...]
    for idx, piece in enumerate(_split3(cum_ref[...])):
        qb = qb + jnp.dot(piece, pq_ref[idx], preferred_element_type=F32)
        kb = kb + jnp.dot(piece, pk_ref[idx], preferred_element_type=F32)
    for h in range(FOX_HEADS):
        lo, hi = h * LANES, (h + 1) * LANES
        base = h * QK_WIDTH
        q_ref[:, base:base + LANES] = fq[:, lo:hi].astype(q_ref.dtype)
        q_ref[:, base + LANES:base + QK_WIDTH] = qb[:, lo:hi].astype(q_ref.dtype)
        k_ref[:, base:base + LANES] = fk[:, lo:hi].astype(k_ref.dtype)
        k_ref[:, base + LANES:base + QK_WIDTH] = kb[:, lo:hi].astype(k_ref.dtype)


def _fox_bias_constants():
    hw = FOX_HEADS * LANES
    pq = np.zeros((3, LANES, hw), np.float32)
    pk = np.zeros((3, LANES, hw), np.float32)
    oq = np.zeros((1, hw), np.float32)
    ok = np.zeros((1, hw), np.float32)
    for h in range(FOX_HEADS):
        for p in range(3):
            pq[p, h, h * LANES + p] = 1.0
            oq[0, h * LANES + 3 + p] = 1.0
            pk[p, h, h * LANES + 3 + p] = -1.0
            ok[0, h * LANES + p] = 1.0
    return (jnp.asarray(pq, BF16), jnp.asarray(pk, BF16), jnp.asarray(oq), jnp.asarray(ok))


def _fox_prep(xn, cum, wq, wk, tm):
    s, d = xn.shape
    hw = FOX_HEADS * LANES
    pq, pk, oq, ok = _fox_bias_constants()
    q_scale = FOX_HEAD_DIM ** -0.5 * LOG2E
    return pl.pallas_call(
        functools.partial(_fox_prep_kernel, q_scale=q_scale),
        grid=(s // tm,),
        in_specs=[pl.BlockSpec((tm, d), lambda i: (i, 0)),
                  pl.BlockSpec((tm, LANES), lambda i: (i, 0)),
                  pl.BlockSpec((d, hw), lambda i: (0, 0)),
                  pl.BlockSpec((d, hw), lambda i: (0, 0)),
                  pl.BlockSpec((3, LANES, hw), lambda i: (0, 0, 0)),
                  pl.BlockSpec((3, LANES, hw), lambda i: (0, 0, 0)),
                  pl.BlockSpec((1, hw), lambda i: (0, 0)),
                  pl.BlockSpec((1, hw), lambda i: (0, 0))],
        out_specs=[pl.BlockSpec((tm, FOX_HEADS * QK_WIDTH), lambda i: (i, 0))] * 2,
        out_shape=[jax.ShapeDtypeStruct((s, FOX_HEADS * QK_WIDTH), BF16)] * 2,
        compiler_params=_params(1),
        name="fox_prep",
    )(xn, cum, wq, wk, pq, pk, oq, ok)


def _attn_kernel(q_ref, k_ref, vt_ref, o_ref, m_ref, l_ref, acc_ref, *, tq, tk):
    i = pl.program_id(1)
    ncol = tq // tk
    m_ref[...] = jnp.full_like(m_ref, -jnp.inf)
    l_ref[...] = jnp.zeros_like(l_ref)
    acc_ref[...] = jnp.zeros_like(acc_ref)

    def kv_tile(start, groups, diag_group):
        k = k_ref[pl.ds(start, tk), :]
        vt = vt_ref[:, pl.ds(start, tk)]
        sts = [lax.dot_general(k, q_ref[c * tk:(c + 1) * tk, :], (((1,), (1,)), ((), ())),
                               preferred_element_type=F32) for c in groups]
        for c, st in zip(groups, sts):
            cols = slice(c * tk, (c + 1) * tk)
            if c == diag_group:
                kv_pos = lax.broadcasted_iota(jnp.int32, (tk, tk), 0)
                q_pos = lax.broadcasted_iota(jnp.int32, (tk, tk), 1)
                st = jnp.where(kv_pos <= q_pos, st, -jnp.inf)
            m_prev = m_ref[:, cols]
            m_new = jnp.maximum(m_prev, jnp.max(st, axis=0, keepdims=True))
            alpha = jnp.exp2(m_prev - m_new)
            pt = jnp.exp2(st - m_new)
            l_ref[:, cols] = alpha * l_ref[:, cols] + jnp.sum(pt, axis=0, keepdims=True)
            acc_ref[:, cols] = alpha * acc_ref[:, cols] + jnp.dot(
                vt, pt.astype(BF16), preferred_element_type=F32)
            m_ref[:, cols] = m_new

    def full_tile(j, carry):
        kv_tile(pl.multiple_of(j * tk, tk), list(range(ncol)), None)
        return carry

    lax.fori_loop(0, i * ncol, full_tile, 0)
    for r in range(ncol):
        kv_tile(pl.multiple_of((i * ncol + r) * tk, tk), list(range(r, ncol)), r)
    for c in range(ncol):
        cols = slice(c * tk, (c + 1) * tk)
        o_ref[cols, :] = (acc_ref[:, cols] / l_ref[:, cols]).T.astype(o_ref.dtype)


def _attention(q, k, vt, n_heads, dv, tq, tk, name):
    s = q.shape[0]
    return pl.pallas_call(
        functools.partial(_attn_kernel, tq=tq, tk=tk),
        grid=(n_heads, s // tq),
        in_specs=[pl.BlockSpec((tq, QK_WIDTH), lambda h, i: (i, h)),
                  pl.BlockSpec((s, QK_WIDTH), lambda h, i: (0, h)),
                  pl.BlockSpec((dv, s), lambda h, i: (h, 0))],
        out_specs=pl.BlockSpec((tq, dv), lambda h, i: (i, h)),
        out_shape=jax.ShapeDtypeStruct((s, n_heads * dv), BF16),
        scratch_shapes=[pltpu.VMEM((1, tq), F32), pltpu.VMEM((1, tq), F32),
                        pltpu.VMEM((dv, tq), F32)],
        compiler_params=_params(2),
        name=name,
    )(q, k, vt)


def _merge_kernel(om_ref, of_ref, wm_ref, wf_ref, gm_ref, gf_ref, o_ref):
    pm = jnp.dot(om_ref[...], wm_ref[...], preferred_element_type=F32)
    pf = jnp.dot(of_ref[...], wf_ref[...], preferred_element_type=F32)
    o_ref[...] = (gm_ref[...].astype(F32) * pm + gf_ref[...].astype(F32) * pf).astype(o_ref.dtype)


def _merge(o_mla, o_fox, wm, wf, gates, tm, tn):
    s, kd = o_mla.shape
    n = wm.shape[1]
    nb = n // tn
    return pl.pallas_call(
        _merge_kernel,
        grid=(s // tm, nb),
        in_specs=[pl.BlockSpec((tm, kd), lambda i, j: (i, 0)),
                  pl.BlockSpec((tm, kd), lambda i, j: (i, 0)),
                  pl.BlockSpec((kd, tn), lambda i, j: (0, j)),
                  pl.BlockSpec((kd, tn), lambda i, j: (0, j)),
                  pl.BlockSpec((tm, tn), lambda i, j: (i, j)),
                  pl.BlockSpec((tm, tn), lambda i, j: (i, j + nb))],
        out_specs=pl.BlockSpec((tm, tn), lambda i, j: (i, j)),
        out_shape=jax.ShapeDtypeStruct((s, n), BF16),
        compiler_params=_params(2),
        name="gated_merge",
    )(o_mla, o_fox, wm, wf, gates, gates)


HALO = 16


def _ffn_kernel(h_ref, halo_ref, g_ref, wg_ref, wu_ref, cg_ref, cu_ref, bg_ref, bu_ref,
                wo_ref, gf_ref, y_ref, xn_ref, acc_ref, *, tm):
    i = pl.program_id(0)
    j = pl.program_id(1)

    @pl.when(j == 0)
    def _():
        h = h_ref[...]
        g = g_ref[...]
        keep = jnp.where(i > 0, 1.0, 0.0)
        xn_ref[0:HALO, :] = (_rms(halo_ref[...], g) * keep).astype(xn_ref.dtype)
        xn_ref[HALO:HALO + tm, :] = _rms(h, g).astype(xn_ref.dtype)
        acc_ref[...] = h

    xn = xn_ref[...]

    def conv(w_ref, c_ref, b_ref):
        hh = jnp.dot(xn, w_ref[...], preferred_element_type=F32)
        c = c_ref[...]
        return (hh[HALO:HALO + tm] * c[2:3] + hh[HALO - 1:HALO - 1 + tm] * c[1:2]
                + hh[HALO - 2:HALO - 2 + tm] * c[0:1] + b_ref[...])

    gate = conv(wg_ref, cg_ref, bg_ref)
    up = conv(wu_ref, cu_ref, bu_ref)
    act = (gate * jax.nn.sigmoid(gate) * up).astype(BF16)
    acc_ref[...] += jnp.dot(act, wo_ref[...], preferred_element_type=F32)

    @pl.when(j == pl.num_programs(1) - 1)
    def _():
        y_ref[...] = _rms(acc_ref[...], gf_ref[...]).astype(y_ref.dtype)


def _conv_ffn(h1, g_ffn, w_in, w_conv, b_conv, w_out, g_final, tm, tn):
    s, d = h1.shape
    f = w_out.shape[0]
    nb = f // tn
    hb = tm // HALO
    return pl.pallas_call(
        functools.partial(_ffn_kernel, tm=tm),
        grid=(s // tm, nb),
        in_specs=[pl.BlockSpec((tm, d), lambda i, j: (i, 0)),
                  pl.BlockSpec((HALO, d), lambda i, j: (jnp.maximum(i * hb - 1, 0), 0)),
                  pl.BlockSpec((1, d), lambda i, j: (0, 0)),
                  pl.BlockSpec((d, tn), lambda i, j: (0, j)),
                  pl.BlockSpec((d, tn), lambda i, j: (0, j + nb)),
                  pl.BlockSpec((CONV_WIDTH, tn), lambda i, j: (0, j)),
                  pl.BlockSpec((CONV_WIDTH, tn), lambda i, j: (0, j + nb)),
                  pl.BlockSpec((1, tn), lambda i, j: (0, j)),
                  pl.BlockSpec((1, tn), lambda i, j: (0, j + nb)),
                  pl.BlockSpec((tn, d), lambda i, j: (j, 0)),
                  pl.BlockSpec((1, d), lambda i, j: (0, 0))],
        out_specs=pl.BlockSpec((tm, d), lambda i, j: (i, 0)),
        out_shape=jax.ShapeDtypeStruct((s, d), F32),
        scratch_shapes=[pltpu.VMEM((HALO + tm, d), BF16), pltpu.VMEM((tm, d), F32)],
        compiler_params=_params(2),
        name="conv_ffn",
    )(h1, h1, g_ffn, w_in, w_in, w_conv, w_conv, b_conv, b_conv, w_out, g_final)


def _pad_cols(w, width):
    return jnp.pad(w, ((0, 0), (0, width - w.shape[1])))


def _layer(h, positions, norm_mix, w_in, b_forget, norm_q_a, w_uq, norm_kv_a, w_ukv,
           w_proj_mla, w_proj_fox, w_out, norm_ffn, w_ffn_in, w_conv, b_conv, w_ffn_out,
           final_gain):
    s = h.shape[0]
    tm = min(512, s)
    hw = MLA_HEADS * LANES

    o = np.cumsum([0, MLA_Q_RANK, MLA_KV_RANK, MLA_ROPE_DIM, hw, hw, hw, FOX_HEADS,
                   D_MODEL, D_MODEL])
    w_lat = w_in[:, o[0]:o[2]]
    w_kr = w_in[:, o[2]:o[3]]
    w_kr_swapped = jnp.concatenate([w_kr[:, ROPE_HALF:], w_kr[:, :ROPE_HALF]], axis=1)
    w_small = jnp.concatenate(
        [w_lat, _pad_cols(w_kr, LANES), _pad_cols(w_kr_swapped, LANES),
         _pad_cols(w_in[:, o[6]:o[7]], LANES)], axis=1).astype(BF16)
    w_fq = w_in[:, o[3]:o[4]].astype(BF16)
    w_fk = w_in[:, o[4]:o[5]].astype(BF16)
    w_fvt = w_in[:, o[5]:o[6]].T.astype(BF16)
    w_gates = w_in[:, o[7]:o[9]].astype(BF16)

    uq = w_uq.reshape(MLA_Q_RANK, MLA_HEADS, MLA_NOPE_DIM + MLA_ROPE_DIM)
    w_qn = uq[:, :, :MLA_NOPE_DIM].reshape(MLA_Q_RANK, hw).astype(BF16)
    r1 = uq[:, :, MLA_NOPE_DIM:MLA_NOPE_DIM + ROPE_HALF]
    r2 = uq[:, :, MLA_NOPE_DIM + ROPE_HALF:]
    zpad = jnp.zeros((MLA_Q_RANK, MLA_HEADS, LANES - MLA_ROPE_DIM), w_uq.dtype)
    w_qa = jnp.concatenate([r1, r2, zpad], axis=2).reshape(MLA_Q_RANK, hw).astype(BF16)
    w_qb = jnp.concatenate([r2, r1, zpad], axis=2).reshape(MLA_Q_RANK, hw).astype(BF16)
    ukv = w_ukv.reshape(MLA_KV_RANK, MLA_HEADS, MLA_NOPE_DIM + MLA_V_DIM)
    w_kn = ukv[:, :, :MLA_NOPE_DIM].reshape(MLA_KV_RANK, hw).astype(BF16)
    w_vt = ukv[:, :, MLA_NOPE_DIM:].reshape(MLA_KV_RANK, hw).T.astype(BF16)

    xn = _rmsnorm(h, norm_mix[None, :], tm)
    small = _matmul(xn, w_small, F32, tm, w_small.shape[1], name="proj_latents")
    fvt = _matmul_t(w_fvt, xn, BF16, tm, name="proj_fox_v")
    gates = _matmul(xn, w_gates, BF16, tm, 1024, epilogue=jax.nn.sigmoid, name="proj_gates")

    per_row = LANES // ROPE_HALF
    inv_freq = ROPE_THETA ** (-jnp.arange(0, MLA_ROPE_DIM, 2, dtype=F32) / MLA_ROPE_DIM)
    pos_dense = jnp.repeat(positions, ROPE_HALF).reshape(s // per_row, LANES)
    invf_dense = jnp.tile(inv_freq, per_row)[None, :]
    cos_d, sin_d, nsin_d = _rope_tables(pos_dense, invf_dense, min(512, s // per_row))
    cos, sin, nsin = (t.reshape(s, ROPE_HALF) for t in (cos_d, sin_d, nsin_d))
    zeros = jnp.zeros((s, LANES - MLA_ROPE_DIM), F32)
    ct = jnp.concatenate([cos, cos, zeros], axis=1)
    st = jnp.concatenate([nsin, sin, zeros], axis=1)

    q_mla, k_mla, vt_mla = _mla_prep(small, ct, st, norm_q_a[None, :], norm_kv_a[None, :],
                                     w_qn, w_qa, w_qb, w_kn, w_vt, tm)

    fl_col = (MLA_Q_RANK + MLA_KV_RANK) // LANES + 2
    cum = _decay_cumsum(small, fl_col, _pad_cols(b_forget[None, :], LANES), min(256, s))
    q_fox, k_fox = _fox_prep(xn, cum, w_fq, w_fk, tm)

    tq, tk = min(2048, s), 512
    o_mla = _attention(q_mla, k_mla, vt_mla, MLA_HEADS, MLA_V_DIM, tq, tk, "attn_mla")
    o_fox = _attention(q_fox, k_fox, fvt, FOX_HEADS, FOX_HEAD_DIM, tq, tk, "attn_fox")

    merged = _merge(o_mla, o_fox, w_proj_mla.astype(BF16), w_proj_fox.astype(BF16), gates,
                    tm, 1024)
    h1 = _matmul(merged, w_out.astype(BF16), F32, tm, 1024,
                 epilogue=lambda acc, res: acc + res, extras=(h,), name="proj_out")

    return _conv_ffn(h1, norm_ffn[None, :], w_ffn_in.astype(BF16), w_conv, b_conv[None, :],
                     w_ffn_out.astype(BF16), final_gain[None, :], tm, 512)


def kernel(x, positions, norm_mix, w_in, b_forget, norm_q_a, w_uq, norm_kv_a, w_ukv, w_proj_mla,
           w_proj_fox, w_out, norm_ffn, w_ffn_in, w_conv, b_conv, w_ffn_out, norm_final):
    batch, _, _ = x.shape
    depth = w_in.shape[0]
    assert depth == 1, "the fused ConvFFN folds the final norm into the only layer"
    outs = []
    for b in range(batch):
        y = _layer(x[b], positions[b], norm_mix[0], w_in[0], b_forget[0], norm_q_a[0], w_uq[0],
                   norm_kv_a[0], w_ukv[0], w_proj_mla[0], w_proj_fox[0], w_out[0], norm_ffn[0],
                   w_ffn_in[0], w_conv[0], b_conv[0], w_ffn_out[0], norm_final)
        outs.append(y)
    return jnp.stack(outs, axis=0)
```

```python
import functools
import math

import jax
import jax.numpy as jnp
import numpy as np
from jax import lax
from jax.experimental import pallas as pl
from jax.experimental.pallas import tpu as pltpu

F32 = jnp.float32
BF16 = jnp.bfloat16

D_MODEL = 2048
MLA_HEADS = 8
MLA_Q_RANK = 512
MLA_KV_RANK = 512
MLA_NOPE_DIM = 128
MLA_ROPE_DIM = 64
MLA_V_DIM = 128
FOX_HEADS = 8
FOX_HEAD_DIM = 128
D_FF = 5632
CONV_WIDTH = 3
ROPE_THETA = 10000.0
NORM_EPS = 1e-6

LANES = 128
QK_WIDTH = 2 * LANES
ROPE_HALF = MLA_ROPE_DIM // 2
LOG2E = math.log2(math.e)
VMEM_LIMIT_BYTES = 56 * 1024 * 1024


def _params(n_grid_dims):
    return pltpu.CompilerParams(
        dimension_semantics=("arbitrary",) * n_grid_dims,
        vmem_limit_bytes=VMEM_LIMIT_BYTES)


def _rms(x, g):
    y = x * lax.rsqrt(jnp.mean(x * x, axis=-1, keepdims=True) + NORM_EPS)
    return y * g


def _rmsnorm_kernel(x_ref, g_ref, o_ref):
    o_ref[...] = _rms(x_ref[...], g_ref[...]).astype(o_ref.dtype)


def _rmsnorm(x, g, tm):
    s, d = x.shape
    return pl.pallas_call(
        _rmsnorm_kernel,
        grid=(s // tm,),
        in_specs=[pl.BlockSpec((tm, d), lambda i: (i, 0)),
                  pl.BlockSpec((1, d), lambda i: (0, 0))],
        out_specs=pl.BlockSpec((tm, d), lambda i: (i, 0)),
        out_shape=jax.ShapeDtypeStruct((s, d), BF16),
        compiler_params=_params(1),
        name="rmsnorm",
    )(x, g)


def _mm_kernel(a_ref, w_ref, *rest, epilogue):
    o_ref = rest[-1]
    acc = jnp.dot(a_ref[...], w_ref[...], preferred_element_type=F32)
    if epilogue is not None:
        acc = epilogue(acc, *[r[...] for r in rest[:-1]])
    o_ref[...] = acc.astype(o_ref.dtype)


def _matmul(a, w, out_dtype, tm, tn, epilogue=None, extras=(), name="matmul"):
    m, k = a.shape
    n = w.shape[1]
    in_specs = [pl.BlockSpec((tm, k), lambda i, j: (i, 0)),
                pl.BlockSpec((k, tn), lambda i, j: (0, j))]
    in_specs += [pl.BlockSpec((tm, tn), lambda i, j: (i, j)) for _ in extras]
    return pl.pallas_call(
        functools.partial(_mm_kernel, epilogue=epilogue),
        grid=(m // tm, n // tn),
        in_specs=in_specs,
        out_specs=pl.BlockSpec((tm, tn), lambda i, j: (i, j)),
        out_shape=jax.ShapeDtypeStruct((m, n), out_dtype),
        compiler_params=_params(2),
        name=name,
    )(a, w, *extras)


def _mm_t_kernel(wt_ref, a_ref, o_ref):
    o_ref[...] = lax.dot_general(wt_ref[...], a_ref[...], (((1,), (1,)), ((), ())),
                                 preferred_element_type=F32).astype(o_ref.dtype)


def _matmul_t(wt, a, out_dtype, tm, name):
    n, k = wt.shape
    m = a.shape[0]
    return pl.pallas_call(
        _mm_t_kernel,
        grid=(m // tm,),
        in_specs=[pl.BlockSpec((n, k), lambda i: (0, 0)),
                  pl.BlockSpec((tm, k), lambda i: (i, 0))],
        out_specs=pl.BlockSpec((n, tm), lambda i: (0, i)),
        out_shape=jax.ShapeDtypeStruct((n, m), out_dtype),
        compiler_params=_params(1),
        name=name,
    )(wt, a)


def _rope_table_kernel(pos_ref, invf_ref, cos_ref, sin_ref, nsin_ref):
    ang = pos_ref[...].astype(F32) * invf_ref[...]
    s = jnp.sin(ang)
    cos_ref[...] = jnp.cos(ang)
    sin_ref[...] = s
    nsin_ref[...] = -s


def _rope_tables(pos_dense, invf_dense, tb):
    r = pos_dense.shape[0]
    spec = pl.BlockSpec((tb, LANES), lambda i: (i, 0))
    return pl.pallas_call(
        _rope_table_kernel,
        grid=(r // tb,),
        in_specs=[spec, pl.BlockSpec((1, LANES), lambda i: (0, 0))],
        out_specs=[spec, spec, spec],
        out_shape=[jax.ShapeDtypeStruct((r, LANES), F32)] * 3,
        compiler_params=_params(1),
        name="rope_tables",
    )(pos_dense, invf_dense)


def _split3(x):
    p1 = x.astype(BF16)
    r1 = x - p1.astype(F32)
    p2 = r1.astype(BF16)
    p3 = (r1 - p2.astype(F32)).astype(BF16)
    return p1, p2, p3


def _decay_kernel(fl_ref, b_ref, o_ref, carry_ref, *, tb):
    @pl.when(pl.program_id(0) == 0)
    def _():
        carry_ref[...] = jnp.zeros_like(carry_ref)

    z = fl_ref[...] + b_ref[...]
    logf = jnp.minimum(z, 0.0) - jnp.log1p(jnp.exp(-jnp.abs(z)))
    row = lax.broadcasted_iota(jnp.int32, (tb, tb), 0)
    col = lax.broadcasted_iota(jnp.int32, (tb, tb), 1)
    tri = (col <= row).astype(BF16)
    c = carry_ref[...]
    for piece in _split3(logf):
        c = c + jnp.dot(tri, piece, preferred_element_type=F32)
    o_ref[...] = c * LOG2E
    carry_ref[...] = c[tb - 1:tb, :]


def _decay_cumsum(small, fl_col, b_tile, tb):
    s = small.shape[0]
    return pl.pallas_call(
        functools.partial(_decay_kernel, tb=tb),
        grid=(s // tb,),
        in_specs=[pl.BlockSpec((tb, LANES), lambda i: (i, fl_col)),
                  pl.BlockSpec((1, LANES), lambda i: (0, 0))],
        out_specs=pl.BlockSpec((tb, LANES), lambda i: (i, 0)),
        out_shape=jax.ShapeDtypeStruct((s, LANES), F32),
        scratch_shapes=[pltpu.VMEM((1, LANES), F32)],
        compiler_params=_params(1),
        name="decay_cumsum",
    )(small, b_tile)


def _mla_prep_kernel(cq_ref, ckv_ref, ka_ref, kb_ref, ct_ref, st_ref, gq_ref, gkv_ref,
                     wqn_ref, wqa_ref, wqb_ref, wkn_ref, wvt_ref,
                     q_ref, k_ref, vt_ref, *, q_scale):
    ct = ct_ref[...]
    st = st_ref[...]
    cq = _rms(cq_ref[...], gq_ref[...]).astype(BF16)
    ckv = _rms(ckv_ref[...], gkv_ref[...]).astype(BF16)
    qn = jnp.dot(cq, wqn_ref[...], preferred_element_type=F32)
    qa = jnp.dot(cq, wqa_ref[...], preferred_element_type=F32)
    qb = jnp.dot(cq, wqb_ref[...], preferred_element_type=F32)
    kn = jnp.dot(ckv, wkn_ref[...], preferred_element_type=F32)
    vt_ref[...] = lax.dot_general(wvt_ref[...], ckv, (((1,), (1,)), ((), ())),
                                  preferred_element_type=F32).astype(vt_ref.dtype)
    kr = (ka_ref[...] * ct + kb_ref[...] * st).astype(k_ref.dtype)
    for h in range(MLA_HEADS):
        lo, hi = h * LANES, (h + 1) * LANES
        base = h * QK_WIDTH
        q_ref[:, base:base + LANES] = (qn[:, lo:hi] * q_scale).astype(q_ref.dtype)
        q_ref[:, base + LANES:base + QK_WIDTH] = (
            (qa[:, lo:hi] * ct + qb[:, lo:hi] * st) * q_scale).astype(q_ref.dtype)
        k_ref[:, base:base + LANES] = kn[:, lo:hi].astype(k_ref.dtype)
        k_ref[:, base + LANES:base + QK_WIDTH] = kr


def _mla_prep(small, ct, st, gq, gkv, wqn, wqa, wqb, wkn, wvt, tm):
    s = small.shape[0]
    hw = MLA_HEADS * LANES
    full = lambda shape: pl.BlockSpec(shape, lambda i: (0, 0))
    rows = lambda width, col: pl.BlockSpec((tm, width), lambda i, c=col: (i, c))
    q_scale = (MLA_NOPE_DIM + MLA_ROPE_DIM) ** -0.5 * LOG2E
    return pl.pallas_call(
        functools.partial(_mla_prep_kernel, q_scale=q_scale),
        grid=(s // tm,),
        in_specs=[rows(MLA_Q_RANK, 0), rows(MLA_KV_RANK, 1),
                  rows(LANES, (MLA_Q_RANK + MLA_KV_RANK) // LANES),
                  rows(LANES, (MLA_Q_RANK + MLA_KV_RANK) // LANES + 1),
                  rows(LANES, 0), rows(LANES, 0),
                  full((1, MLA_Q_RANK)), full((1, MLA_KV_RANK)),
                  full((MLA_Q_RANK, hw)), full((MLA_Q_RANK, hw)), full((MLA_Q_RANK, hw)),
                  full((MLA_KV_RANK, hw)), full((hw, MLA_KV_RANK))],
        out_specs=[pl.BlockSpec((tm, MLA_HEADS * QK_WIDTH), lambda i: (i, 0)),
                   pl.BlockSpec((tm, MLA_HEADS * QK_WIDTH), lambda i: (i, 0)),
                   pl.BlockSpec((hw, tm), lambda i: (0, i))],
        out_shape=[jax.ShapeDtypeStruct((s, MLA_HEADS * QK_WIDTH), BF16),
                   jax.ShapeDtypeStruct((s, MLA_HEADS * QK_WIDTH), BF16),
                   jax.ShapeDtypeStruct((hw, s), BF16)],
        compiler_params=_params(1),
        name="mla_prep",
    )(small, small, small, small, ct, st, gq, gkv, wqn, wqa, wqb, wkn, wvt)


def _fox_prep_kernel(xn_ref, cum_ref, wq_ref, wk_ref, pq_ref, pk_ref, oq_ref, ok_ref,
                     q_ref, k_ref, *, q_scale):
    xn = xn_ref[...]
    fq = jnp.dot(xn, wq_ref[...], preferred_element_type=F32) * q_scale
    fk = jnp.dot(xn, wk_ref[...], preferred_element_type=F32)
    qb = oq_ref[...]
    kb = ok_ref[...]
    for idx, piece in enumerate(_split3(cum_ref[...])):
        qb = qb + jnp.dot(piece, pq_ref[idx], preferred_element_type=F32)
        kb = kb + jnp.dot(piece, pk_ref[idx], preferred_element_type=F32)
    for h in range(FOX_HEADS):
        lo, hi = h * LANES, (h + 1) * LANES
        base = h * QK_WIDTH
        q_ref[:, base:base + LANES] = fq[:, lo:hi].astype(q_ref.dtype)
        q_ref[:, base + LANES:base + QK_WIDTH] = qb[:, lo:hi].astype(q_ref.dtype)
        k_ref[:, base:base + LANES] = fk[:, lo:hi].astype(k_ref.dtype)
        k_ref[:, base + LANES:base + QK_WIDTH] = kb[:, lo:hi].astype(k_ref.dtype)


def _fox_bias_constants():
    hw = FOX_HEADS * LANES
    pq = np.zeros((3, LANES, hw), np.float32)
    pk = np.zeros((3, LANES, hw), np.float32)
    oq = np.zeros((1, hw), np.float32)
    ok = np.zeros((1, hw), np.float32)
    for h in range(FOX_HEADS):
        for p in range(3):
            pq[p, h, h * LANES + p] = 1.0
            oq[0, h * LANES + 3 + p] = 1.0
            pk[p, h, h * LANES + 3 + p] = -1.0
            ok[0, h * LANES + p] = 1.0
    return (jnp.asarray(pq, BF16), jnp.asarray(pk, BF16), jnp.asarray(oq), jnp.asarray(ok))


def _fox_prep(xn, cum, wq, wk, tm):
    s, d = xn.shape
    hw = FOX_HEADS * LANES
    pq, pk, oq, ok = _fox_bias_constants()
    q_scale = FOX_HEAD_DIM ** -0.5 * LOG2E
    return pl.pallas_call(
        functools.partial(_fox_prep_kernel, q_scale=q_scale),
        grid=(s // tm,),
        in_specs=[pl.BlockSpec((tm, d), lambda i: (i, 0)),
                  pl.BlockSpec((tm, LANES), lambda i: (i, 0)),
                  pl.BlockSpec((d, hw), lambda i: (0, 0)),
                  pl.BlockSpec((d, hw), lambda i: (0, 0)),
                  pl.BlockSpec((3, LANES, hw), lambda i: (0, 0, 0)),
                  pl.BlockSpec((3, LANES, hw), lambda i: (0, 0, 0)),
                  pl.BlockSpec((1, hw), lambda i: (0, 0)),
                  pl.BlockSpec((1, hw), lambda i: (0, 0))],
        out_specs=[pl.BlockSpec((tm, FOX_HEADS * QK_WIDTH), lambda i: (i, 0))] * 2,
        out_shape=[jax.ShapeDtypeStruct((s, FOX_HEADS * QK_WIDTH), BF16)] * 2,
        compiler_params=_params(1),
        name="fox_prep",
    )(xn, cum, wq, wk, pq, pk, oq, ok)


def _attn_kernel(q_ref, k_ref, vt_ref, o_ref, m_ref, l_ref, acc_ref, sa_ref, sb_ref,
                 mxa_ref, mxb_ref, *, tq, tk):
    i = pl.program_id(1)
    ncol = tq // tk
    m_ref[...] = jnp.full_like(m_ref, -jnp.inf)
    l_ref[...] = jnp.zeros_like(l_ref)
    acc_ref[...] = jnp.zeros_like(acc_ref)

    def logits(start, groups, buf):
        s_ref, mx_ref = buf
        k = k_ref[pl.ds(pl.multiple_of(start, tk), tk), :]
        for c in groups:
            st = lax.dot_general(k, q_ref[c * tk:(c + 1) * tk, :],
                                 (((1,), (1,)), ((), ())), preferred_element_type=F32)
            s_ref[c] = st
            mx_ref[c] = jnp.max(st, axis=0, keepdims=True)

    def softmax_pv(start, groups, diag_group, buf):
        s_ref, mx_ref = buf
        vt = vt_ref[:, pl.ds(pl.multiple_of(start, tk), tk)]
        for c in groups:
            cols = slice(c * tk, (c + 1) * tk)
            st = s_ref[c]
            if c == diag_group:
                kv_pos = lax.broadcasted_iota(jnp.int32, (tk, tk), 0)
                q_pos = lax.broadcasted_iota(jnp.int32, (tk, tk), 1)
                st = jnp.where(kv_pos <= q_pos, st, -jnp.inf)
                tile_max = jnp.max(st, axis=0, keepdims=True)
            else:
                tile_max = mx_ref[c]
            m_prev = m_ref[:, cols]
            m_new = jnp.maximum(m_prev, tile_max)
            alpha = jnp.exp2(m_prev - m_new)
            pt = jnp.exp2(st - m_new)
            l_ref[:, cols] = alpha * l_ref[:, cols] + jnp.sum(pt, axis=0, keepdims=True)
            acc_ref[:, cols] = alpha * acc_ref[:, cols] + jnp.dot(
                vt, pt.astype(BF16), preferred_element_type=F32)
            m_ref[:, cols] = m_new

    every = list(range(ncol))
    n_full = i * ncol
    buf_a, buf_b = (sa_ref, mxa_ref), (sb_ref, mxb_ref)
    logits(0, every, buf_a)

    def tile_pair(u, carry):
        t0 = 2 * u
        logits((t0 + 1) * tk, every, buf_b)
        softmax_pv(t0 * tk, every, None, buf_a)
        logits((t0 + 2) * tk, every, buf_a)
        softmax_pv((t0 + 1) * tk, every, None, buf_b)
        return carry

    lax.fori_loop(0, n_full // 2, tile_pair, 0)
    bufs = (buf_a, buf_b)
    base = n_full * tk
    for r in range(ncol):
        if r + 1 < ncol:
            logits(base + (r + 1) * tk, list(range(r + 1, ncol)), bufs[(r + 1) % 2])
        softmax_pv(base + r * tk, list(range(r, ncol)), r, bufs[r % 2])
    for c in range(ncol):
        cols = slice(c * tk, (c + 1) * tk)
        o_ref[cols, :] = (acc_ref[:, cols] / l_ref[:, cols]).T.astype(o_ref.dtype)


def _attention(q, k, vt, n_heads, dv, tq, tk, name):
    s = q.shape[0]
    ncol = tq // tk
    assert tq % tk == 0 and ncol % 2 == 0, "the kv pipeline assumes an even tile count per q block"
    return pl.pallas_call(
        functools.partial(_attn_kernel, tq=tq, tk=tk),
        grid=(n_heads, s // tq),
        in_specs=[pl.BlockSpec((tq, QK_WIDTH), lambda h, i: (i, h)),
                  pl.BlockSpec((s, QK_WIDTH), lambda h, i: (0, h)),
                  pl.BlockSpec((dv, s), lambda h, i: (h, 0))],
        out_specs=pl.BlockSpec((tq, dv), lambda h, i: (i, h)),
        out_shape=jax.ShapeDtypeStruct((s, n_heads * dv), BF16),
        scratch_shapes=[pltpu.VMEM((1, tq), F32), pltpu.VMEM((1, tq), F32),
                        pltpu.VMEM((dv, tq), F32),
                        pltpu.VMEM((ncol, tk, tk), F32), pltpu.VMEM((ncol, tk, tk), F32),
                        pltpu.VMEM((ncol, 1, tk), F32), pltpu.VMEM((ncol, 1, tk), F32)],
        compiler_params=_params(2),
        name=name,
    )(q, k, vt)


def _merge_kernel(om_ref, of_ref, wm_ref, wf_ref, gm_ref, gf_ref, o_ref):
    pm = jnp.dot(om_ref[...], wm_ref[...], preferred_element_type=F32)
    pf = jnp.dot(of_ref[...], wf_ref[...], preferred_element_type=F32)
    o_ref[...] = (gm_ref[...].astype(F32) * pm + gf_ref[...].astype(F32) * pf).astype(o_ref.dtype)


def _merge(o_mla, o_fox, wm, wf, gates, tm, tn):
    s, kd = o_mla.shape
    n = wm.shape[1]
    nb = n // tn
    return pl.pallas_call(
        _merge_kernel,
        grid=(s // tm, nb),
        in_specs=[pl.BlockSpec((tm, kd), lambda i, j: (i, 0)),
                  pl.BlockSpec((tm, kd), lambda i, j: (i, 0)),
                  pl.BlockSpec((kd, tn), lambda i, j: (0, j)),
                  pl.BlockSpec((kd, tn), lambda i, j: (0, j)),
                  pl.BlockSpec((tm, tn), lambda i, j: (i, j)),
                  pl.BlockSpec((tm, tn), lambda i, j: (i, j + nb))],
        out_specs=pl.BlockSpec((tm, tn), lambda i, j: (i, j)),
        out_shape=jax.ShapeDtypeStruct((s, n), BF16),
        compiler_params=_params(2),
        name="gated_merge",
    )(o_mla, o_fox, wm, wf, gates, gates)


HALO = 16


def _ffn_kernel(h_ref, halo_ref, g_ref, wg_ref, wu_ref, cg_ref, cu_ref, bg_ref, bu_ref,
                wo_ref, gf_ref, y_ref, xn_ref, acc_ref, *, tm):
    i = pl.program_id(0)
    j = pl.program_id(1)

    @pl.when(j == 0)
    def _():
        h = h_ref[...]
        g = g_ref[...]
        keep = jnp.where(i > 0, 1.0, 0.0)
        xn_ref[0:HALO, :] = (_rms(halo_ref[...], g) * keep).astype(xn_ref.dtype)
        xn_ref[HALO:HALO + tm, :] = _rms(h, g).astype(xn_ref.dtype)
        acc_ref[...] = h

    xn = xn_ref[...]

    def conv(w_ref, c_ref, b_ref):
        hh = jnp.dot(xn, w_ref[...], preferred_element_type=F32)
        c = c_ref[...]
        return (hh[HALO:HALO + tm] * c[2:3] + hh[HALO - 1:HALO - 1 + tm] * c[1:2]
                + hh[HALO - 2:HALO - 2 + tm] * c[0:1] + b_ref[...])

    gate = conv(wg_ref, cg_ref, bg_ref)
    up = conv(wu_ref, cu_ref, bu_ref)
    act = (gate * jax.nn.sigmoid(gate) * up).astype(BF16)
    acc_ref[...] += jnp.dot(act, wo_ref[...], preferred_element_type=F32)

    @pl.when(j == pl.num_programs(1) - 1)
    def _():
        y_ref[...] = _rms(acc_ref[...], gf_ref[...]).astype(y_ref.dtype)


def _conv_ffn(h1, g_ffn, w_in, w_conv, b_conv, w_out, g_final, tm, tn):
    s, d = h1.shape
    f = w_out.shape[0]
    nb = f // tn
    hb = tm // HALO
    return pl.pallas_call(
        functools.partial(_ffn_kernel, tm=tm),
        grid=(s // tm, nb),
        in_specs=[pl.BlockSpec((tm, d), lambda i, j: (i, 0)),
                  pl.BlockSpec((HALO, d), lambda i, j: (jnp.maximum(i * hb - 1, 0), 0)),
                  pl.BlockSpec((1, d), lambda i, j: (0, 0)),
                  pl.BlockSpec((d, tn), lambda i, j: (0, j)),
                  pl.BlockSpec((d, tn), lambda i, j: (0, j + nb)),
                  pl.BlockSpec((CONV_WIDTH, tn), lambda i, j: (0, j)),
                  pl.BlockSpec((CONV_WIDTH, tn), lambda i, j: (0, j + nb)),
                  pl.BlockSpec((1, tn), lambda i, j: (0, j)),
                  pl.BlockSpec((1, tn), lambda i, j: (0, j + nb)),
                  pl.BlockSpec((tn, d), lambda i, j: (j, 0)),
                  pl.BlockSpec((1, d), lambda i, j: (0, 0))],
        out_specs=pl.BlockSpec((tm, d), lambda i, j: (i, 0)),
        out_shape=jax.ShapeDtypeStruct((s, d), F32),
        scratch_shapes=[pltpu.VMEM((HALO + tm, d), BF16), pltpu.VMEM((tm, d), F32)],
        compiler_params=_params(2),
        name="conv_ffn",
    )(h1, h1, g_ffn, w_in, w_in, w_conv, w_conv, b_conv, b_conv, w_out, g_final)


def _pad_cols(w, width):
    return jnp.pad(w, ((0, 0), (0, width - w.shape[1])))


def _layer(h, positions, norm_mix, w_in, b_forget, norm_q_a, w_uq, norm_kv_a, w_ukv,
           w_proj_mla, w_proj_fox, w_out, norm_ffn, w_ffn_in, w_conv, b_conv, w_ffn_out,
           final_gain):
    s = h.shape[0]
    tm = min(512, s)
    tm_mm = min(1024, s)
    hw = MLA_HEADS * LANES

    o = np.cumsum([0, MLA_Q_RANK, MLA_KV_RANK, MLA_ROPE_DIM, hw, hw, hw, FOX_HEADS,
                   D_MODEL, D_MODEL])
    w_lat = w_in[:, o[0]:o[2]]
    w_kr = w_in[:, o[2]:o[3]]
    w_kr_swapped = jnp.concatenate([w_kr[:, ROPE_HALF:], w_kr[:, :ROPE_HALF]], axis=1)
    w_small = jnp.concatenate(
        [w_lat, _pad_cols(w_kr, LANES), _pad_cols(w_kr_swapped, LANES),
         _pad_cols(w_in[:, o[6]:o[7]], LANES)], axis=1).astype(BF16)
    w_fq = w_in[:, o[3]:o[4]].astype(BF16)
    w_fk = w_in[:, o[4]:o[5]].astype(BF16)
    w_fvt = w_in[:, o[5]:o[6]].T.astype(BF16)
    w_gates = w_in[:, o[7]:o[9]].astype(BF16)

    uq = w_uq.reshape(MLA_Q_RANK, MLA_HEADS, MLA_NOPE_DIM + MLA_ROPE_DIM)
    w_qn = uq[:, :, :MLA_NOPE_DIM].reshape(MLA_Q_RANK, hw).astype(BF16)
    r1 = uq[:, :, MLA_NOPE_DIM:MLA_NOPE_DIM + ROPE_HALF]
    r2 = uq[:, :, MLA_NOPE_DIM + ROPE_HALF:]
    zpad = jnp.zeros((MLA_Q_RANK, MLA_HEADS, LANES - MLA_ROPE_DIM), w_uq.dtype)
    w_qa = jnp.concatenate([r1, r2, zpad], axis=2).reshape(MLA_Q_RANK, hw).astype(BF16)
    w_qb = jnp.concatenate([r2, r1, zpad], axis=2).reshape(MLA_Q_RANK, hw).astype(BF16)
    ukv = w_ukv.reshape(MLA_KV_RANK, MLA_HEADS, MLA_NOPE_DIM + MLA_V_DIM)
    w_kn = ukv[:, :, :MLA_NOPE_DIM].reshape(MLA_KV_RANK, hw).astype(BF16)
    w_vt = ukv[:, :, MLA_NOPE_DIM:].reshape(MLA_KV_RANK, hw).T.astype(BF16)

    xn = _rmsnorm(h, norm_mix[None, :], tm)
    small = _matmul(xn, w_small, F32, tm_mm, w_small.shape[1], name="proj_latents")
    fvt = _matmul_t(w_fvt, xn, BF16, tm, name="proj_fox_v")
    gates = _matmul(xn, w_gates, BF16, tm_mm, 1024, epilogue=jax.nn.sigmoid, name="proj_gates")

    per_row = LANES // ROPE_HALF
    inv_freq = ROPE_THETA ** (-jnp.arange(0, MLA_ROPE_DIM, 2, dtype=F32) / MLA_ROPE_DIM)
    pos_dense = jnp.repeat(positions, ROPE_HALF).reshape(s // per_row, LANES)
    invf_dense = jnp.tile(inv_freq, per_row)[None, :]
    cos_d, sin_d, nsin_d = _rope_tables(pos_dense, invf_dense, min(512, s // per_row))
    cos, sin, nsin = (t.reshape(s, ROPE_HALF) for t in (cos_d, sin_d, nsin_d))
    zeros = jnp.zeros((s, LANES - MLA_ROPE_DIM), F32)
    ct = jnp.concatenate([cos, cos, zeros], axis=1)
    st = jnp.concatenate([nsin, sin, zeros], axis=1)

    q_mla, k_mla, vt_mla = _mla_prep(small, ct, st, norm_q_a[None, :], norm_kv_a[None, :],
                                     w_qn, w_qa, w_qb, w_kn, w_vt, tm)

    fl_col = (MLA_Q_RANK + MLA_KV_RANK) // LANES + 2
    cum = _decay_cumsum(small, fl_col, _pad_cols(b_forget[None, :], LANES), min(256, s))
    q_fox, k_fox = _fox_prep(xn, cum, w_fq, w_fk, tm)

    tq, tk = min(2048, s), 512
    o_mla = _attention(q_mla, k_mla, vt_mla, MLA_HEADS, MLA_V_DIM, tq, tk, "attn_mla")
    o_fox = _attention(q_fox, k_fox, fvt, FOX_HEADS, FOX_HEAD_DIM, tq, tk, "attn_fox")

    merged = _merge(o_mla, o_fox, w_proj_mla.astype(BF16), w_proj_fox.astype(BF16), gates,
                    tm_mm, 1024)
    h1 = _matmul(merged, w_out.astype(BF16), F32, tm_mm, 1024,
                 epilogue=lambda acc, res: acc + res, extras=(h,), name="proj_out")

    return _conv_ffn(h1, norm_ffn[None, :], w_ffn_in.astype(BF16), w_conv, b_conv[None, :],
                     w_ffn_out.astype(BF16), final_gain[None, :], tm, 512)


def kernel(x, positions, norm_mix, w_in, b_forget, norm_q_a, w_uq, norm_kv_a, w_ukv, w_proj_mla,
           w_proj_fox, w_out, norm_ffn, w_ffn_in, w_conv, b_conv, w_ffn_out, norm_final):
    batch, _, _ = x.shape
    depth = w_in.shape[0]
    assert depth == 1, "the fused ConvFFN folds the final norm into the only layer"
    outs = []
    for b in range(batch):
        y = _layer(x[b], positions[b], norm_mix[0], w_in[0], b_forget[0], norm_q_a[0], w_uq[0],
                   norm_kv_a[0], w_ukv[0], w_proj_mla[0], w_proj_fox[0], w_out[0], norm_ffn[0],
                   w_ffn_in[0], w_conv[0], b_conv[0], w_ffn_out[0], norm_final)
        outs.append(y)
    return jnp.stack(outs, axis=0)
```

```python
import functools
import math

import jax
import jax.numpy as jnp
import numpy as np
from jax import lax
from jax.experimental import pallas as pl
from jax.experimental.pallas import tpu as pltpu

F32 = jnp.float32
BF16 = jnp.bfloat16

D_MODEL = 2048
MLA_HEADS = 8
MLA_Q_RANK = 512
MLA_KV_RANK = 512
MLA_NOPE_DIM = 128
MLA_ROPE_DIM = 64
MLA_V_DIM = 128
FOX_HEADS = 8
FOX_HEAD_DIM = 128
D_FF = 5632
CONV_WIDTH = 3
ROPE_THETA = 10000.0
NORM_EPS = 1e-6

LANES = 128
QK_WIDTH = 2 * LANES
ROPE_HALF = MLA_ROPE_DIM // 2
LOG2E = math.log2(math.e)
VMEM_LIMIT_BYTES = 56 * 1024 * 1024


def _params(n_grid_dims):
    return pltpu.CompilerParams(
        dimension_semantics=("arbitrary",) * n_grid_dims,
        vmem_limit_bytes=VMEM_LIMIT_BYTES)


def _rms(x, g):
    y = x * lax.rsqrt(jnp.mean(x * x, axis=-1, keepdims=True) + NORM_EPS)
    return y * g


def _rmsnorm_kernel(x_ref, g_ref, o_ref):
    o_ref[...] = _rms(x_ref[...], g_ref[...]).astype(o_ref.dtype)


def _rmsnorm(x, g, tm):
    s, d = x.shape
    return pl.pallas_call(
        _rmsnorm_kernel,
        grid=(s // tm,),
        in_specs=[pl.BlockSpec((tm, d), lambda i: (i, 0)),
                  pl.BlockSpec((1, d), lambda i: (0, 0))],
        out_specs=pl.BlockSpec((tm, d), lambda i: (i, 0)),
        out_shape=jax.ShapeDtypeStruct((s, d), BF16),
        compiler_params=_params(1),
        name="rmsnorm",
    )(x, g)


def _mm_kernel(a_ref, w_ref, *rest, epilogue):
    o_ref = rest[-1]
    acc = jnp.dot(a_ref[...], w_ref[...], preferred_element_type=F32)
    if epilogue is not None:
        acc = epilogue(acc, *[r[...] for r in rest[:-1]])
    o_ref[...] = acc.astype(o_ref.dtype)


def _matmul(a, w, out_dtype, tm, tn, epilogue=None, extras=(), name="matmul"):
    m, k = a.shape
    n = w.shape[1]
    in_specs = [pl.BlockSpec((tm, k), lambda i, j: (i, 0)),
                pl.BlockSpec((k, tn), lambda i, j: (0, j))]
    in_specs += [pl.BlockSpec((tm, tn), lambda i, j: (i, j)) for _ in extras]
    return pl.pallas_call(
        functools.partial(_mm_kernel, epilogue=epilogue),
        grid=(m // tm, n // tn),
        in_specs=in_specs,
        out_specs=pl.BlockSpec((tm, tn), lambda i, j: (i, j)),
        out_shape=jax.ShapeDtypeStruct((m, n), out_dtype),
        compiler_params=_params(2),
        name=name,
    )(a, w, *extras)


def _mm_t_kernel(wt_ref, a_ref, o_ref):
    o_ref[...] = lax.dot_general(wt_ref[...], a_ref[...], (((1,), (1,)), ((), ())),
                                 preferred_element_type=F32).astype(o_ref.dtype)


def _matmul_t(wt, a, out_dtype, tm, name):
    n, k = wt.shape
    m = a.shape[0]
    return pl.pallas_call(
        _mm_t_kernel,
        grid=(m // tm,),
        in_specs=[pl.BlockSpec((n, k), lambda i: (0, 0)),
                  pl.BlockSpec((tm, k), lambda i: (i, 0))],
        out_specs=pl.BlockSpec((n, tm), lambda i: (0, i)),
        out_shape=jax.ShapeDtypeStruct((n, m), out_dtype),
        compiler_params=_params(1),
        name=name,
    )(wt, a)


def _rope_table_kernel(pos_ref, invf_ref, cos_ref, sin_ref, nsin_ref):
    ang = pos_ref[...].astype(F32) * invf_ref[...]
    s = jnp.sin(ang)
    cos_ref[...] = jnp.cos(ang)
    sin_ref[...] = s
    nsin_ref[...] = -s


def _rope_tables(pos_dense, invf_dense, tb):
    r = pos_dense.shape[0]
    spec = pl.BlockSpec((tb, LANES), lambda i: (i, 0))
    return pl.pallas_call(
        _rope_table_kernel,
        grid=(r // tb,),
        in_specs=[spec, pl.BlockSpec((1, LANES), lambda i: (0, 0))],
        out_specs=[spec, spec, spec],
        out_shape=[jax.ShapeDtypeStruct((r, LANES), F32)] * 3,
        compiler_params=_params(1),
        name="rope_tables",
    )(pos_dense, invf_dense)


def _split3(x):
    p1 = x.astype(BF16)
    r1 = x - p1.astype(F32)
    p2 = r1.astype(BF16)
    p3 = (r1 - p2.astype(F32)).astype(BF16)
    return p1, p2, p3


def _decay_kernel(fl_ref, b_ref, o_ref, carry_ref, *, tb):
    @pl.when(pl.program_id(0) == 0)
    def _():
        carry_ref[...] = jnp.zeros_like(carry_ref)

    z = fl_ref[...] + b_ref[...]
    logf = jnp.minimum(z, 0.0) - jnp.log1p(jnp.exp(-jnp.abs(z)))
    row = lax.broadcasted_iota(jnp.int32, (tb, tb), 0)
    col = lax.broadcasted_iota(jnp.int32, (tb, tb), 1)
    tri = (col <= row).astype(BF16)
    c = carry_ref[...]
    for piece in _split3(logf):
        c = c + jnp.dot(tri, piece, preferred_element_type=F32)
    o_ref[...] = c * LOG2E
    carry_ref[...] = c[tb - 1:tb, :]


def _decay_cumsum(small, fl_col, b_tile, tb):
    s = small.shape[0]
    return pl.pallas_call(
        functools.partial(_decay_kernel, tb=tb),
        grid=(s // tb,),
        in_specs=[pl.BlockSpec((tb, LANES), lambda i: (i, fl_col)),
                  pl.BlockSpec((1, LANES), lambda i: (0, 0))],
        out_specs=pl.BlockSpec((tb, LANES), lambda i: (i, 0)),
        out_shape=jax.ShapeDtypeStruct((s, LANES), F32),
        scratch_shapes=[pltpu.VMEM((1, LANES), F32)],
        compiler_params=_params(1),
        name="decay_cumsum",
    )(small, b_tile)


def _mla_prep_kernel(cq_ref, ckv_ref, ka_ref, kb_ref, ct_ref, st_ref, gq_ref, gkv_ref,
                     wqn_ref, wqa_ref, wqb_ref, wkn_ref, wvt_ref,
                     q_ref, k_ref, vt_ref, *, q_scale):
    ct = ct_ref[...]
    st = st_ref[...]
    cq = _rms(cq_ref[...], gq_ref[...]).astype(BF16)
    ckv = _rms(ckv_ref[...], gkv_ref[...]).astype(BF16)
    qn = jnp.dot(cq, wqn_ref[...], preferred_element_type=F32)
    qa = jnp.dot(cq, wqa_ref[...], preferred_element_type=F32)
    qb = jnp.dot(cq, wqb_ref[...], preferred_element_type=F32)
    kn = jnp.dot(ckv, wkn_ref[...], preferred_element_type=F32)
    vt_ref[...] = lax.dot_general(wvt_ref[...], ckv, (((1,), (1,)), ((), ())),
                                  preferred_element_type=F32).astype(vt_ref.dtype)
    kr = (ka_ref[...] * ct + kb_ref[...] * st).astype(k_ref.dtype)
    for h in range(MLA_HEADS):
        lo, hi = h * LANES, (h + 1) * LANES
        base = h * QK_WIDTH
        q_ref[:, base:base + LANES] = (qn[:, lo:hi] * q_scale).astype(q_ref.dtype)
        q_ref[:, base + LANES:base + QK_WIDTH] = (
            (qa[:, lo:hi] * ct + qb[:, lo:hi] * st) * q_scale).astype(q_ref.dtype)
        k_ref[:, base:base + LANES] = kn[:, lo:hi].astype(k_ref.dtype)
        k_ref[:, base + LANES:base + QK_WIDTH] = kr


def _mla_prep(small, ct, st, gq, gkv, wqn, wqa, wqb, wkn, wvt, tm):
    s = small.shape[0]
    hw = MLA_HEADS * LANES
    full = lambda shape: pl.BlockSpec(shape, lambda i: (0, 0))
    rows = lambda width, col: pl.BlockSpec((tm, width), lambda i, c=col: (i, c))
    q_scale = (MLA_NOPE_DIM + MLA_ROPE_DIM) ** -0.5 * LOG2E
    return pl.pallas_call(
        functools.partial(_mla_prep_kernel, q_scale=q_scale),
        grid=(s // tm,),
        in_specs=[rows(MLA_Q_RANK, 0), rows(MLA_KV_RANK, 1),
                  rows(LANES, (MLA_Q_RANK + MLA_KV_RANK) // LANES),
                  rows(LANES, (MLA_Q_RANK + MLA_KV_RANK) // LANES + 1),
                  rows(LANES, 0), rows(LANES, 0),
                  full((1, MLA_Q_RANK)), full((1, MLA_KV_RANK)),
                  full((MLA_Q_RANK, hw)), full((MLA_Q_RANK, hw)), full((MLA_Q_RANK, hw)),
                  full((MLA_KV_RANK, hw)), full((hw, MLA_KV_RANK))],
        out_specs=[pl.BlockSpec((tm, MLA_HEADS * QK_WIDTH), lambda i: (i, 0)),
                   pl.BlockSpec((tm, MLA_HEADS * QK_WIDTH), lambda i: (i, 0)),
                   pl.BlockSpec((hw, tm), lambda i: (0, i))],
        out_shape=[jax.ShapeDtypeStruct((s, MLA_HEADS * QK_WIDTH), BF16),
                   jax.ShapeDtypeStruct((s, MLA_HEADS * QK_WIDTH), BF16),
                   jax.ShapeDtypeStruct((hw, s), BF16)],
        compiler_params=_params(1),
        name="mla_prep",
    )(small, small, small, small, ct, st, gq, gkv, wqn, wqa, wqb, wkn, wvt)


def _fox_prep_kernel(xn_ref, cum_ref, wq_ref, wk_ref, place_ref, ones_ref,
                     q_ref, k_ref, *, q_scale):
    hw = FOX_HEADS * LANES
    xn = xn_ref[...]
    fq = jnp.dot(xn, wq_ref[...], preferred_element_type=F32) * q_scale
    fk = jnp.dot(xn, wk_ref[...], preferred_element_type=F32)
    pieces = jnp.concatenate(_split3(cum_ref[...]), axis=1)
    bias = jnp.dot(pieces, place_ref[...], preferred_element_type=F32) + ones_ref[...]
    qb, kb = bias[:, :hw], bias[:, hw:]
    for h in range(FOX_HEADS):
        lo, hi = h * LANES, (h + 1) * LANES
        base = h * QK_WIDTH
        q_ref[:, base:base + LANES] = fq[:, lo:hi].astype(q_ref.dtype)
        q_ref[:, base + LANES:base + QK_WIDTH] = qb[:, lo:hi].astype(q_ref.dtype)
        k_ref[:, base:base + LANES] = fk[:, lo:hi].astype(k_ref.dtype)
        k_ref[:, base + LANES:base + QK_WIDTH] = kb[:, lo:hi].astype(k_ref.dtype)


def _fox_bias_constants():
    hw = FOX_HEADS * LANES
    place = np.zeros((3 * LANES, 2 * hw), np.float32)
    ones = np.zeros((1, 2 * hw), np.float32)
    for h in range(FOX_HEADS):
        for p in range(3):
            place[p * LANES + h, h * LANES + p] = 1.0
            ones[0, h * LANES + 3 + p] = 1.0
            place[p * LANES + h, hw + h * LANES + 3 + p] = -1.0
            ones[0, hw + h * LANES + p] = 1.0
    return jnp.asarray(place, BF16), jnp.asarray(ones)


def _fox_prep(xn, cum, wq, wk, tm):
    s, d = xn.shape
    hw = FOX_HEADS * LANES
    place, ones = _fox_bias_constants()
    q_scale = FOX_HEAD_DIM ** -0.5 * LOG2E
    return pl.pallas_call(
        functools.partial(_fox_prep_kernel, q_scale=q_scale),
        grid=(s // tm,),
        in_specs=[pl.BlockSpec((tm, d), lambda i: (i, 0)),
                  pl.BlockSpec((tm, LANES), lambda i: (i, 0)),
                  pl.BlockSpec((d, hw), lambda i: (0, 0)),
                  pl.BlockSpec((d, hw), lambda i: (0, 0)),
                  pl.BlockSpec((3 * LANES, 2 * hw), lambda i: (0, 0)),
                  pl.BlockSpec((1, 2 * hw), lambda i: (0, 0))],
        out_specs=[pl.BlockSpec((tm, FOX_HEADS * QK_WIDTH), lambda i: (i, 0))] * 2,
        out_shape=[jax.ShapeDtypeStruct((s, FOX_HEADS * QK_WIDTH), BF16)] * 2,
        compiler_params=_params(1),
        name="fox_prep",
    )(xn, cum, wq, wk, place, ones)


def _attn_kernel(q_ref, k_ref, vt_ref, o_ref, m_ref, l_ref, acc_ref, sa_ref, sb_ref,
                 mxa_ref, mxb_ref, *, tq, tk):
    i = pl.program_id(1)
    ncol = tq // tk
    m_ref[...] = jnp.full_like(m_ref, -jnp.inf)
    l_ref[...] = jnp.zeros_like(l_ref)
    acc_ref[...] = jnp.zeros_like(acc_ref)

    def logits(start, groups, buf):
        s_ref, mx_ref = buf
        k = k_ref[pl.ds(pl.multiple_of(start, tk), tk), :]
        for c in groups:
            st = lax.dot_general(k, q_ref[c * tk:(c + 1) * tk, :],
                                 (((1,), (1,)), ((), ())), preferred_element_type=F32)
            s_ref[c] = st
            mx_ref[c] = jnp.max(st, axis=0, keepdims=True)

    def softmax_pv(start, groups, diag_group, buf):
        s_ref, mx_ref = buf
        vt = vt_ref[:, pl.ds(pl.multiple_of(start, tk), tk)]
        for c in groups:
            cols = slice(c * tk, (c + 1) * tk)
            st = s_ref[c]
            if c == diag_group:
                kv_pos = lax.broadcasted_iota(jnp.int32, (tk, tk), 0)
                q_pos = lax.broadcasted_iota(jnp.int32, (tk, tk), 1)
                st = jnp.where(kv_pos <= q_pos, st, -jnp.inf)
                tile_max = jnp.max(st, axis=0, keepdims=True)
            else:
                tile_max = mx_ref[c]
            m_prev = m_ref[:, cols]
            m_new = jnp.maximum(m_prev, tile_max)
            alpha = jnp.exp2(m_prev - m_new)
            pt = jnp.exp2(st - m_new)
            l_ref[:, cols] = alpha * l_ref[:, cols] + jnp.sum(pt, axis=0, keepdims=True)
            acc_ref[:, cols] = alpha * acc_ref[:, cols] + jnp.dot(
                vt, pt.astype(BF16), preferred_element_type=F32)
            m_ref[:, cols] = m_new

    every = list(range(ncol))
    n_full = i * ncol
    buf_a, buf_b = (sa_ref, mxa_ref), (sb_ref, mxb_ref)
    logits(0, every, buf_a)

    def tile_pair(u, carry):
        t0 = 2 * u
        logits((t0 + 1) * tk, every, buf_b)
        softmax_pv(t0 * tk, every, None, buf_a)
        logits((t0 + 2) * tk, every, buf_a)
        softmax_pv((t0 + 1) * tk, every, None, buf_b)
        return carry

    lax.fori_loop(0, n_full // 2, tile_pair, 0)
    bufs = (buf_a, buf_b)
    base = n_full * tk
    for r in range(ncol):
        if r + 1 < ncol:
            logits(base + (r + 1) * tk, list(range(r + 1, ncol)), bufs[(r + 1) % 2])
        softmax_pv(base + r * tk, list(range(r, ncol)), r, bufs[r % 2])
    for c in range(ncol):
        cols = slice(c * tk, (c + 1) * tk)
        o_ref[cols, :] = (acc_ref[:, cols] / l_ref[:, cols]).T.astype(o_ref.dtype)


def _attention(q, k, vt, n_heads, dv, tq, tk, name):
    s = q.shape[0]
    ncol = tq // tk
    assert tq % tk == 0 and ncol % 2 == 0, "the kv pipeline assumes an even tile count per q block"
    return pl.pallas_call(
        functools.partial(_attn_kernel, tq=tq, tk=tk),
        grid=(n_heads, s // tq),
        in_specs=[pl.BlockSpec((tq, QK_WIDTH), lambda h, i: (i, h)),
                  pl.BlockSpec((s, QK_WIDTH), lambda h, i: (0, h)),
                  pl.BlockSpec((dv, s), lambda h, i: (h, 0))],
        out_specs=pl.BlockSpec((tq, dv), lambda h, i: (i, h)),
        out_shape=jax.ShapeDtypeStruct((s, n_heads * dv), BF16),
        scratch_shapes=[pltpu.VMEM((1, tq), F32), pltpu.VMEM((1, tq), F32),
                        pltpu.VMEM((dv, tq), F32),
                        pltpu.VMEM((ncol, tk, tk), F32), pltpu.VMEM((ncol, tk, tk), F32),
                        pltpu.VMEM((ncol, 1, tk), F32), pltpu.VMEM((ncol, 1, tk), F32)],
        compiler_params=_params(2),
        name=name,
    )(q, k, vt)


def _merge_kernel(om_ref, of_ref, wm_ref, wf_ref, gm_ref, gf_ref, o_ref):
    pm = jnp.dot(om_ref[...], wm_ref[...], preferred_element_type=F32)
    pf = jnp.dot(of_ref[...], wf_ref[...], preferred_element_type=F32)
    o_ref[...] = (gm_ref[...].astype(F32) * pm + gf_ref[...].astype(F32) * pf).astype(o_ref.dtype)


def _merge(o_mla, o_fox, wm, wf, gates, tm, tn):
    s, kd = o_mla.shape
    n = wm.shape[1]
    nb = n // tn
    return pl.pallas_call(
        _merge_kernel,
        grid=(s // tm, nb),
        in_specs=[pl.BlockSpec((tm, kd), lambda i, j: (i, 0)),
                  pl.BlockSpec((tm, kd), lambda i, j: (i, 0)),
                  pl.BlockSpec((kd, tn), lambda i, j: (0, j)),
                  pl.BlockSpec((kd, tn), lambda i, j: (0, j)),
                  pl.BlockSpec((tm, tn), lambda i, j: (i, j)),
                  pl.BlockSpec((tm, tn), lambda i, j: (i, j + nb))],
        out_specs=pl.BlockSpec((tm, tn), lambda i, j: (i, j)),
        out_shape=jax.ShapeDtypeStruct((s, n), BF16),
        compiler_params=_params(2),
        name="gated_merge",
    )(o_mla, o_fox, wm, wf, gates, gates)


HALO = 8


def _ffn_kernel(h_ref, g_ref, wg_ref, wu_ref, cg_ref, cu_ref, bg_ref, bu_ref,
                wo_ref, gf_ref, y_ref, xn_ref, acc_ref, hg_ref, hu_ref, tail_g_ref, tail_u_ref,
                *, tm, n_split):
    i = pl.program_id(0)
    j = pl.program_id(1)
    half = wg_ref.shape[1] // n_split

    @pl.when(jnp.logical_and(i == 0, j == 0))
    def _():
        tail_g_ref[...] = jnp.zeros_like(tail_g_ref)
        tail_u_ref[...] = jnp.zeros_like(tail_u_ref)

    @pl.when(j == 0)
    def _():
        h = h_ref[...]
        xn_ref[...] = _rms(h, g_ref[...]).astype(xn_ref.dtype)
        acc_ref[...] = h

    xn = xn_ref[...]
    for n in range(n_split):
        cols = slice(n * half, (n + 1) * half)
        for hh_ref, tail_ref, w_ref in ((hg_ref, tail_g_ref, wg_ref), (hu_ref, tail_u_ref, wu_ref)):
            hh_ref[n, 0:HALO, :] = tail_ref[j, n]
            hh = jnp.dot(xn, w_ref[:, cols], preferred_element_type=F32)
            hh_ref[n, HALO:HALO + tm, :] = hh
            tail_ref[j, n] = hh[tm - HALO:tm, :]

    def conv(hh_ref, n, c_ref, b_ref):
        cols = slice(n * half, (n + 1) * half)
        c = c_ref[:, cols]
        return (hh_ref[n, HALO:HALO + tm, :] * c[2:3]
                + hh_ref[n, HALO - 1:HALO - 1 + tm, :] * c[1:2]
                + hh_ref[n, HALO - 2:HALO - 2 + tm, :] * c[0:1] + b_ref[:, cols])

    for n in range(n_split):
        gate = conv(hg_ref, n, cg_ref, bg_ref)
        up = conv(hu_ref, n, cu_ref, bu_ref)
        act = (gate * jax.nn.sigmoid(gate) * up).astype(BF16)
        acc_ref[...] += jnp.dot(act, wo_ref[n * half:(n + 1) * half, :],
                                preferred_element_type=F32)

    @pl.when(j == pl.num_programs(1) - 1)
    def _():
        y_ref[...] = _rms(acc_ref[...], gf_ref[...]).astype(y_ref.dtype)


def _conv_ffn(h1, g_ffn, w_in, w_conv, b_conv, w_out, g_final, tm, tn):
    s, d = h1.shape
    f = w_out.shape[0]
    nb = f // tn
    n_split = 2
    half = tn // n_split
    return pl.pallas_call(
        functools.partial(_ffn_kernel, tm=tm, n_split=n_split),
        grid=(s // tm, nb),
        in_specs=[pl.BlockSpec((tm, d), lambda i, j: (i, 0)),
                  pl.BlockSpec((1, d), lambda i, j: (0, 0)),
                  pl.BlockSpec((d, tn), lambda i, j: (0, j)),
                  pl.BlockSpec((d, tn), lambda i, j: (0, j + nb)),
                  pl.BlockSpec((CONV_WIDTH, tn), lambda i, j: (0, j)),
                  pl.BlockSpec((CONV_WIDTH, tn), lambda i, j: (0, j + nb)),
                  pl.BlockSpec((1, tn), lambda i, j: (0, j)),
                  pl.BlockSpec((1, tn), lambda i, j: (0, j + nb)),
                  pl.BlockSpec((tn, d), lambda i, j: (j, 0)),
                  pl.BlockSpec((1, d), lambda i, j: (0, 0))],
        out_specs=pl.BlockSpec((tm, d), lambda i, j: (i, 0)),
        out_shape=jax.ShapeDtypeStruct((s, d), F32),
        scratch_shapes=[pltpu.VMEM((tm, d), BF16), pltpu.VMEM((tm, d), F32),
                        pltpu.VMEM((n_split, HALO + tm, half), F32),
                        pltpu.VMEM((n_split, HALO + tm, half), F32),
                        pltpu.VMEM((nb, n_split, HALO, half), F32),
                        pltpu.VMEM((nb, n_split, HALO, half), F32)],
        compiler_params=_params(2),
        name="conv_ffn",
    )(h1, g_ffn, w_in, w_in, w_conv, w_conv, b_conv, b_conv, w_out, g_final)


def _pad_cols(w, width):
    return jnp.pad(w, ((0, 0), (0, width - w.shape[1])))


def _layer(h, positions, norm_mix, w_in, b_forget, norm_q_a, w_uq, norm_kv_a, w_ukv,
           w_proj_mla, w_proj_fox, w_out, norm_ffn, w_ffn_in, w_conv, b_conv, w_ffn_out,
           final_gain):
    s = h.shape[0]
    tm = min(512, s)
    tm_mm = min(1024, s)
    hw = MLA_HEADS * LANES

    o = np.cumsum([0, MLA_Q_RANK, MLA_KV_RANK, MLA_ROPE_DIM, hw, hw, hw, FOX_HEADS,
                   D_MODEL, D_MODEL])
    w_lat = w_in[:, o[0]:o[2]]
    w_kr = w_in[:, o[2]:o[3]]
    w_kr_swapped = jnp.concatenate([w_kr[:, ROPE_HALF:], w_kr[:, :ROPE_HALF]], axis=1)
    w_small = jnp.concatenate(
        [w_lat, _pad_cols(w_kr, LANES), _pad_cols(w_kr_swapped, LANES),
         _pad_cols(w_in[:, o[6]:o[7]], LANES)], axis=1).astype(BF16)
    w_fq = w_in[:, o[3]:o[4]].astype(BF16)
    w_fk = w_in[:, o[4]:o[5]].astype(BF16)
    w_fvt = w_in[:, o[5]:o[6]].T.astype(BF16)
    w_gates = w_in[:, o[7]:o[9]].astype(BF16)

    uq = w_uq.reshape(MLA_Q_RANK, MLA_HEADS, MLA_NOPE_DIM + MLA_ROPE_DIM)
    w_qn = uq[:, :, :MLA_NOPE_DIM].reshape(MLA_Q_RANK, hw).astype(BF16)
    r1 = uq[:, :, MLA_NOPE_DIM:MLA_NOPE_DIM + ROPE_HALF]
    r2 = uq[:, :, MLA_NOPE_DIM + ROPE_HALF:]
    zpad = jnp.zeros((MLA_Q_RANK, MLA_HEADS, LANES - MLA_ROPE_DIM), w_uq.dtype)
    w_qa = jnp.concatenate([r1, r2, zpad], axis=2).reshape(MLA_Q_RANK, hw).astype(BF16)
    w_qb = jnp.concatenate([r2, r1, zpad], axis=2).reshape(MLA_Q_RANK, hw).astype(BF16)
    ukv = w_ukv.reshape(MLA_KV_RANK, MLA_HEADS, MLA_NOPE_DIM + MLA_V_DIM)
    w_kn = ukv[:, :, :MLA_NOPE_DIM].reshape(MLA_KV_RANK, hw).astype(BF16)
    w_vt = ukv[:, :, MLA_NOPE_DIM:].reshape(MLA_KV_RANK, hw).T.astype(BF16)

    xn = _rmsnorm(h, norm_mix[None, :], tm)
    small = _matmul(xn, w_small, F32, tm_mm, w_small.shape[1], name="proj_latents")
    fvt = _matmul_t(w_fvt, xn, BF16, tm, name="proj_fox_v")
    gates = _matmul(xn, w_gates, BF16, tm_mm, 1024, epilogue=jax.nn.sigmoid, name="proj_gates")

    per_row = LANES // ROPE_HALF
    inv_freq = ROPE_THETA ** (-jnp.arange(0, MLA_ROPE_DIM, 2, dtype=F32) / MLA_ROPE_DIM)
    pos_dense = jnp.repeat(positions, ROPE_HALF).reshape(s // per_row, LANES)
    invf_dense = jnp.tile(inv_freq, per_row)[None, :]
    cos_d, sin_d, nsin_d = _rope_tables(pos_dense, invf_dense, min(512, s // per_row))
    cos, sin, nsin = (t.reshape(s, ROPE_HALF) for t in (cos_d, sin_d, nsin_d))
    zeros = jnp.zeros((s, LANES - MLA_ROPE_DIM), F32)
    ct = jnp.concatenate([cos, cos, zeros], axis=1)
    st = jnp.concatenate([nsin, sin, zeros], axis=1)

    q_mla, k_mla, vt_mla = _mla_prep(small, ct, st, norm_q_a[None, :], norm_kv_a[None, :],
                                     w_qn, w_qa, w_qb, w_kn, w_vt, tm)

    fl_col = (MLA_Q_RANK + MLA_KV_RANK) // LANES + 2
    cum = _decay_cumsum(small, fl_col, _pad_cols(b_forget[None, :], LANES), min(256, s))
    q_fox, k_fox = _fox_prep(xn, cum, w_fq, w_fk, tm)

    tq, tk = min(2048, s), 512
    o_mla = _attention(q_mla, k_mla, vt_mla, MLA_HEADS, MLA_V_DIM, tq, tk, "attn_mla")
    o_fox = _attention(q_fox, k_fox, fvt, FOX_HEADS, FOX_HEAD_DIM, tq, tk, "attn_fox")

    merged = _merge(o_mla, o_fox, w_proj_mla.astype(BF16), w_proj_fox.astype(BF16), gates,
                    tm_mm, 1024)
    h1 = _matmul(merged, w_out.astype(BF16), F32, tm_mm, 1024,
                 epilogue=lambda acc, res: acc + res, extras=(h,), name="proj_out")

    return _conv_ffn(h1, norm_ffn[None, :], w_ffn_in.astype(BF16), w_conv, b_conv[None, :],
                     w_ffn_out.astype(BF16), final_gain[None, :], tm, 512)


def kernel(x, positions, norm_mix, w_in, b_forget, norm_q_a, w_uq, norm_kv_a, w_ukv, w_proj_mla,
           w_proj_fox, w_out, norm_ffn, w_ffn_in, w_conv, b_conv, w_ffn_out, norm_final):
    batch, _, _ = x.shape
    depth = w_in.shape[0]
    assert depth == 1, "the fused ConvFFN folds the final norm into the only layer"
    outs = []
    for b in range(batch):
        y = _layer(x[b], positions[b], norm_mix[0], w_in[0], b_forget[0], norm_q_a[0], w_uq[0],
                   norm_kv_a[0], w_ukv[0], w_proj_mla[0], w_proj_fox[0], w_out[0], norm_ffn[0],
                   w_ffn_in[0], w_conv[0], b_conv[0], w_ffn_out[0], norm_final)
        outs.append(y[None])
    return outs[0] if batch == 1 else jnp.concatenate(outs, axis=0)
```

```python
import functools
import math

import jax
import jax.numpy as jnp
import numpy as np
from jax import lax
from jax.experimental import pallas as pl
from jax.experimental.pallas import tpu as pltpu

F32 = jnp.float32
BF16 = jnp.bfloat16

D_MODEL = 2048
MLA_HEADS = 8
MLA_Q_RANK = 512
MLA_KV_RANK = 512
MLA_NOPE_DIM = 128
MLA_ROPE_DIM = 64
MLA_V_DIM = 128
FOX_HEADS = 8
FOX_HEAD_DIM = 128
D_FF = 5632
CONV_WIDTH = 3
ROPE_THETA = 10000.0
NORM_EPS = 1e-6

LANES = 128
QK_WIDTH = 2 * LANES
ROPE_HALF = MLA_ROPE_DIM // 2
LOG2E = math.log2(math.e)
VMEM_LIMIT_BYTES = 56 * 1024 * 1024


def _params(n_grid_dims):
    return pltpu.CompilerParams(
        dimension_semantics=("arbitrary",) * n_grid_dims,
        vmem_limit_bytes=VMEM_LIMIT_BYTES)


def _rms(x, g):
    y = x * lax.rsqrt(jnp.mean(x * x, axis=-1, keepdims=True) + NORM_EPS)
    return y * g


def _norm_proj_kernel(x_ref, g_ref, w_ref, xn_ref, o_ref):
    xn = _rms(x_ref[...], g_ref[...]).astype(xn_ref.dtype)
    xn_ref[...] = xn
    o_ref[...] = jnp.dot(xn, w_ref[...], preferred_element_type=F32).astype(o_ref.dtype)


def _norm_proj(x, g, w, tm):
    s, d = x.shape
    n = w.shape[1]
    return pl.pallas_call(
        _norm_proj_kernel,
        grid=(s // tm,),
        in_specs=[pl.BlockSpec((tm, d), lambda i: (i, 0)),
                  pl.BlockSpec((1, d), lambda i: (0, 0)),
                  pl.BlockSpec((d, n), lambda i: (0, 0))],
        out_specs=[pl.BlockSpec((tm, d), lambda i: (i, 0)),
                   pl.BlockSpec((tm, n), lambda i: (i, 0))],
        out_shape=[jax.ShapeDtypeStruct((s, d), BF16), jax.ShapeDtypeStruct((s, n), F32)],
        compiler_params=_params(1),
        name="norm_proj_latents",
    )(x, g, w)


def _mm_kernel(a_ref, w_ref, *rest, epilogue):
    o_ref = rest[-1]
    acc = jnp.dot(a_ref[...], w_ref[...], preferred_element_type=F32)
    if epilogue is not None:
        acc = epilogue(acc, *[r[...] for r in rest[:-1]])
    o_ref[...] = acc.astype(o_ref.dtype)


def _matmul(a, w, out_dtype, tm, tn, epilogue=None, extras=(), name="matmul"):
    m, k = a.shape
    n = w.shape[1]
    in_specs = [pl.BlockSpec((tm, k), lambda i, j: (i, 0)),
                pl.BlockSpec((k, tn), lambda i, j: (0, j))]
    in_specs += [pl.BlockSpec((tm, tn), lambda i, j: (i, j)) for _ in extras]
    return pl.pallas_call(
        functools.partial(_mm_kernel, epilogue=epilogue),
        grid=(m // tm, n // tn),
        in_specs=in_specs,
        out_specs=pl.BlockSpec((tm, tn), lambda i, j: (i, j)),
        out_shape=jax.ShapeDtypeStruct((m, n), out_dtype),
        compiler_params=_params(2),
        name=name,
    )(a, w, *extras)


def _mm_t_kernel(wt_ref, a_ref, o_ref):
    o_ref[...] = lax.dot_general(wt_ref[...], a_ref[...], (((1,), (1,)), ((), ())),
                                 preferred_element_type=F32).astype(o_ref.dtype)


def _matmul_t(wt, a, out_dtype, tm, name):
    n, k = wt.shape
    m = a.shape[0]
    return pl.pallas_call(
        _mm_t_kernel,
        grid=(m // tm,),
        in_specs=[pl.BlockSpec((n, k), lambda i: (0, 0)),
                  pl.BlockSpec((tm, k), lambda i: (i, 0))],
        out_specs=pl.BlockSpec((n, tm), lambda i: (0, i)),
        out_shape=jax.ShapeDtypeStruct((n, m), out_dtype),
        compiler_params=_params(1),
        name=name,
    )(wt, a)


def _rope_table_kernel(pos_ref, invf_ref, cos_ref, sin_ref, nsin_ref):
    ang = pos_ref[...].astype(F32) * invf_ref[...]
    s = jnp.sin(ang)
    cos_ref[...] = jnp.cos(ang)
    sin_ref[...] = s
    nsin_ref[...] = -s


def _rope_tables(pos_dense, invf_dense, tb):
    r = pos_dense.shape[0]
    spec = pl.BlockSpec((tb, LANES), lambda i: (i, 0))
    return pl.pallas_call(
        _rope_table_kernel,
        grid=(r // tb,),
        in_specs=[spec, pl.BlockSpec((1, LANES), lambda i: (0, 0))],
        out_specs=[spec, spec, spec],
        out_shape=[jax.ShapeDtypeStruct((r, LANES), F32)] * 3,
        compiler_params=_params(1),
        name="rope_tables",
    )(pos_dense, invf_dense)


def _split3(x):
    p1 = x.astype(BF16)
    r1 = x - p1.astype(F32)
    p2 = r1.astype(BF16)
    p3 = (r1 - p2.astype(F32)).astype(BF16)
    return p1, p2, p3


def _decay_kernel(fl_ref, b_ref, o_ref, carry_ref, *, tb):
    @pl.when(pl.program_id(0) == 0)
    def _():
        carry_ref[...] = jnp.zeros_like(carry_ref)

    z = fl_ref[...] + b_ref[...]
    logf = jnp.minimum(z, 0.0) - jnp.log1p(jnp.exp(-jnp.abs(z)))
    row = lax.broadcasted_iota(jnp.int32, (tb, tb), 0)
    col = lax.broadcasted_iota(jnp.int32, (tb, tb), 1)
    tri = (col <= row).astype(BF16)
    c = carry_ref[...]
    for piece in _split3(logf):
        c = c + jnp.dot(tri, piece, preferred_element_type=F32)
    o_ref[...] = c * LOG2E
    carry_ref[...] = c[tb - 1:tb, :]


def _decay_cumsum(small, fl_col, b_tile, tb):
    s = small.shape[0]
    return pl.pallas_call(
        functools.partial(_decay_kernel, tb=tb),
        grid=(s // tb,),
        in_specs=[pl.BlockSpec((tb, LANES), lambda i: (i, fl_col)),
                  pl.BlockSpec((1, LANES), lambda i: (0, 0))],
        out_specs=pl.BlockSpec((tb, LANES), lambda i: (i, 0)),
        out_shape=jax.ShapeDtypeStruct((s, LANES), F32),
        scratch_shapes=[pltpu.VMEM((1, LANES), F32)],
        compiler_params=_params(1),
        name="decay_cumsum",
    )(small, b_tile)


def _mla_prep_kernel(cq_ref, ckv_ref, ka_ref, kb_ref, ct_ref, st_ref, gq_ref, gkv_ref,
                     wqn_ref, wqa_ref, wqb_ref, wkn_ref, wvt_ref,
                     q_ref, k_ref, vt_ref, *, q_scale):
    ct = ct_ref[...]
    st = st_ref[...]
    cq = _rms(cq_ref[...], gq_ref[...]).astype(BF16)
    ckv = _rms(ckv_ref[...], gkv_ref[...]).astype(BF16)
    qn = jnp.dot(cq, wqn_ref[...], preferred_element_type=F32)
    qa = jnp.dot(cq, wqa_ref[...], preferred_element_type=F32)
    qb = jnp.dot(cq, wqb_ref[...], preferred_element_type=F32)
    kn = jnp.dot(ckv, wkn_ref[...], preferred_element_type=F32)
    vt_ref[...] = lax.dot_general(wvt_ref[...], ckv, (((1,), (1,)), ((), ())),
                                  preferred_element_type=F32).astype(vt_ref.dtype)
    kr = (ka_ref[...] * ct + kb_ref[...] * st).astype(k_ref.dtype)
    for h in range(MLA_HEADS):
        lo, hi = h * LANES, (h + 1) * LANES
        base = h * QK_WIDTH
        q_ref[:, base:base + LANES] = (qn[:, lo:hi] * q_scale).astype(q_ref.dtype)
        q_ref[:, base + LANES:base + QK_WIDTH] = (
            (qa[:, lo:hi] * ct + qb[:, lo:hi] * st) * q_scale).astype(q_ref.dtype)
        k_ref[:, base:base + LANES] = kn[:, lo:hi].astype(k_ref.dtype)
        k_ref[:, base + LANES:base + QK_WIDTH] = kr


def _mla_prep(small, ct, st, gq, gkv, wqn, wqa, wqb, wkn, wvt, tm):
    s = small.shape[0]
    hw = MLA_HEADS * LANES
    full = lambda shape: pl.BlockSpec(shape, lambda i: (0, 0))
    rows = lambda width, col: pl.BlockSpec((tm, width), lambda i, c=col: (i, c))
    q_scale = (MLA_NOPE_DIM + MLA_ROPE_DIM) ** -0.5 * LOG2E
    return pl.pallas_call(
        functools.partial(_mla_prep_kernel, q_scale=q_scale),
        grid=(s // tm,),
        in_specs=[rows(MLA_Q_RANK, 0), rows(MLA_KV_RANK, 1),
                  rows(LANES, (MLA_Q_RANK + MLA_KV_RANK) // LANES),
                  rows(LANES, (MLA_Q_RANK + MLA_KV_RANK) // LANES + 1),
                  rows(LANES, 0), rows(LANES, 0),
                  full((1, MLA_Q_RANK)), full((1, MLA_KV_RANK)),
                  full((MLA_Q_RANK, hw)), full((MLA_Q_RANK, hw)), full((MLA_Q_RANK, hw)),
                  full((MLA_KV_RANK, hw)), full((hw, MLA_KV_RANK))],
        out_specs=[pl.BlockSpec((tm, MLA_HEADS * QK_WIDTH), lambda i: (i, 0)),
                   pl.BlockSpec((tm, MLA_HEADS * QK_WIDTH), lambda i: (i, 0)),
                   pl.BlockSpec((hw, tm), lambda i: (0, i))],
        out_shape=[jax.ShapeDtypeStruct((s, MLA_HEADS * QK_WIDTH), BF16),
                   jax.ShapeDtypeStruct((s, MLA_HEADS * QK_WIDTH), BF16),
                   jax.ShapeDtypeStruct((hw, s), BF16)],
        compiler_params=_params(1),
        name="mla_prep",
    )(small, small, small, small, ct, st, gq, gkv, wqn, wqa, wqb, wkn, wvt)


def _fox_prep_kernel(xn_ref, cum_ref, wq_ref, wk_ref, place_ref, ones_ref,
                     q_ref, k_ref, *, q_scale):
    hw = FOX_HEADS * LANES
    xn = xn_ref[...]
    fq = jnp.dot(xn, wq_ref[...], preferred_element_type=F32) * q_scale
    fk = jnp.dot(xn, wk_ref[...], preferred_element_type=F32)
    pieces = jnp.concatenate(_split3(cum_ref[...]), axis=1)
    bias = jnp.dot(pieces, place_ref[...], preferred_element_type=F32) + ones_ref[...]
    qb, kb = bias[:, :hw], bias[:, hw:]
    for h in range(FOX_HEADS):
        lo, hi = h * LANES, (h + 1) * LANES
        base = h * QK_WIDTH
        q_ref[:, base:base + LANES] = fq[:, lo:hi].astype(q_ref.dtype)
        q_ref[:, base + LANES:base + QK_WIDTH] = qb[:, lo:hi].astype(q_ref.dtype)
        k_ref[:, base:base + LANES] = fk[:, lo:hi].astype(k_ref.dtype)
        k_ref[:, base + LANES:base + QK_WIDTH] = kb[:, lo:hi].astype(k_ref.dtype)


def _fox_bias_constants():
    hw = FOX_HEADS * LANES
    place = np.zeros((3 * LANES, 2 * hw), np.float32)
    ones = np.zeros((1, 2 * hw), np.float32)
    for h in range(FOX_HEADS):
        for p in range(3):
            place[p * LANES + h, h * LANES + p] = 1.0
            ones[0, h * LANES + 3 + p] = 1.0
            place[p * LANES + h, hw + h * LANES + 3 + p] = -1.0
            ones[0, hw + h * LANES + p] = 1.0
    return jnp.asarray(place, BF16), jnp.asarray(ones)


def _fox_prep(xn, cum, wq, wk, tm):
    s, d = xn.shape
    hw = FOX_HEADS * LANES
    place, ones = _fox_bias_constants()
    q_scale = FOX_HEAD_DIM ** -0.5 * LOG2E
    return pl.pallas_call(
        functools.partial(_fox_prep_kernel, q_scale=q_scale),
        grid=(s // tm,),
        in_specs=[pl.BlockSpec((tm, d), lambda i: (i, 0)),
                  pl.BlockSpec((tm, LANES), lambda i: (i, 0)),
                  pl.BlockSpec((d, hw), lambda i: (0, 0)),
                  pl.BlockSpec((d, hw), lambda i: (0, 0)),
                  pl.BlockSpec((3 * LANES, 2 * hw), lambda i: (0, 0)),
                  pl.BlockSpec((1, 2 * hw), lambda i: (0, 0))],
        out_specs=[pl.BlockSpec((tm, FOX_HEADS * QK_WIDTH), lambda i: (i, 0))] * 2,
        out_shape=[jax.ShapeDtypeStruct((s, FOX_HEADS * QK_WIDTH), BF16)] * 2,
        compiler_params=_params(1),
        name="fox_prep",
    )(xn, cum, wq, wk, place, ones)


def _attn_kernel(q_ref, k_ref, vt_ref, o_ref, m_ref, l_ref, acc_ref, sa_ref, sb_ref,
                 mxa_ref, mxb_ref, *, tq, tk):
    i = pl.program_id(1)
    ncol = tq // tk
    m_ref[...] = jnp.full_like(m_ref, -jnp.inf)
    l_ref[...] = jnp.zeros_like(l_ref)
    acc_ref[...] = jnp.zeros_like(acc_ref)

    def logits(start, groups, buf):
        s_ref, mx_ref = buf
        k = k_ref[pl.ds(pl.multiple_of(start, tk), tk), :]
        for c in groups:
            st = lax.dot_general(k, q_ref[c * tk:(c + 1) * tk, :],
                                 (((1,), (1,)), ((), ())), preferred_element_type=F32)
            s_ref[c] = st
            mx_ref[c] = jnp.max(st, axis=0, keepdims=True)

    def softmax_pv(start, groups, diag_group, buf):
        s_ref, mx_ref = buf
        vt = vt_ref[:, pl.ds(pl.multiple_of(start, tk), tk)]
        for c in groups:
            cols = slice(c * tk, (c + 1) * tk)
            st = s_ref[c]
            if c == diag_group:
                kv_pos = lax.broadcasted_iota(jnp.int32, (tk, tk), 0)
                q_pos = lax.broadcasted_iota(jnp.int32, (tk, tk), 1)
                st = jnp.where(kv_pos <= q_pos, st, -jnp.inf)
                tile_max = jnp.max(st, axis=0, keepdims=True)
            else:
                tile_max = mx_ref[c]
            m_prev = m_ref[:, cols]
            m_new = jnp.maximum(m_prev, tile_max)
            alpha = jnp.exp2(m_prev - m_new)
            pt = jnp.exp2(st - m_new)
            l_ref[:, cols] = alpha * l_ref[:, cols] + jnp.sum(pt, axis=0, keepdims=True)
            acc_ref[:, cols] = alpha * acc_ref[:, cols] + jnp.dot(
                vt, pt.astype(BF16), preferred_element_type=F32)
            m_ref[:, cols] = m_new

    every = list(range(ncol))
    n_full = i * ncol
    buf_a, buf_b = (sa_ref, mxa_ref), (sb_ref, mxb_ref)
    logits(0, every, buf_a)

    def tile_pair(u, carry):
        t0 = 2 * u
        logits((t0 + 1) * tk, every, buf_b)
        softmax_pv(t0 * tk, every, None, buf_a)
        logits((t0 + 2) * tk, every, buf_a)
        softmax_pv((t0 + 1) * tk, every, None, buf_b)
        return carry

    lax.fori_loop(0, n_full // 2, tile_pair, 0)
    bufs = (buf_a, buf_b)
    base = n_full * tk
    for r in range(ncol):
        if r + 1 < ncol:
            logits(base + (r + 1) * tk, list(range(r + 1, ncol)), bufs[(r + 1) % 2])
        softmax_pv(base + r * tk, list(range(r, ncol)), r, bufs[r % 2])
    for c in range(ncol):
        cols = slice(c * tk, (c + 1) * tk)
        o_ref[cols, :] = (acc_ref[:, cols] / l_ref[:, cols]).T.astype(o_ref.dtype)


def _attention(q, k, vt, n_heads, dv, tq, tk, name):
    s = q.shape[0]
    ncol = tq // tk
    assert tq % tk == 0 and ncol % 2 == 0, "the kv pipeline assumes an even tile count per q block"
    return pl.pallas_call(
        functools.partial(_attn_kernel, tq=tq, tk=tk),
        grid=(n_heads, s // tq),
        in_specs=[pl.BlockSpec((tq, QK_WIDTH), lambda h, i: (i, h)),
                  pl.BlockSpec((s, QK_WIDTH), lambda h, i: (0, h)),
                  pl.BlockSpec((dv, s), lambda h, i: (h, 0))],
        out_specs=pl.BlockSpec((tq, dv), lambda h, i: (i, h)),
        out_shape=jax.ShapeDtypeStruct((s, n_heads * dv), BF16),
        scratch_shapes=[pltpu.VMEM((1, tq), F32), pltpu.VMEM((1, tq), F32),
                        pltpu.VMEM((dv, tq), F32),
                        pltpu.VMEM((ncol, tk, tk), F32), pltpu.VMEM((ncol, tk, tk), F32),
                        pltpu.VMEM((ncol, 1, tk), F32), pltpu.VMEM((ncol, 1, tk), F32)],
        compiler_params=_params(2),
        name=name,
    )(q, k, vt)


def _merge_kernel(om_ref, of_ref, wm_ref, wf_ref, gm_ref, gf_ref, o_ref):
    pm = jnp.dot(om_ref[...], wm_ref[...], preferred_element_type=F32)
    pf = jnp.dot(of_ref[...], wf_ref[...], preferred_element_type=F32)
    o_ref[...] = (gm_ref[...].astype(F32) * pm + gf_ref[...].astype(F32) * pf).astype(o_ref.dtype)


def _merge(o_mla, o_fox, wm, wf, gates, tm, tn):
    s, kd = o_mla.shape
    n = wm.shape[1]
    nb = n // tn
    return pl.pallas_call(
        _merge_kernel,
        grid=(s // tm, nb),
        in_specs=[pl.BlockSpec((tm, kd), lambda i, j: (i, 0)),
                  pl.BlockSpec((tm, kd), lambda i, j: (i, 0)),
                  pl.BlockSpec((kd, tn), lambda i, j: (0, j)),
                  pl.BlockSpec((kd, tn), lambda i, j: (0, j)),
                  pl.BlockSpec((tm, tn), lambda i, j: (i, j)),
                  pl.BlockSpec((tm, tn), lambda i, j: (i, j + nb))],
        out_specs=pl.BlockSpec((tm, tn), lambda i, j: (i, j)),
        out_shape=jax.ShapeDtypeStruct((s, n), BF16),
        compiler_params=_params(2),
        name="gated_merge",
    )(o_mla, o_fox, wm, wf, gates, gates)


HALO = 8


def _ffn_kernel(h_ref, g_ref, wg_ref, wu_ref, cg_ref, cu_ref, bg_ref, bu_ref,
                wo_ref, gf_ref, y_ref, xn_ref, acc_ref, hg_ref, hu_ref, tail_g_ref, tail_u_ref,
                *, tm, n_split):
    i = pl.program_id(0)
    j = pl.program_id(1)
    half = wg_ref.shape[1] // n_split

    @pl.when(jnp.logical_and(i == 0, j == 0))
    def _():
        tail_g_ref[...] = jnp.zeros_like(tail_g_ref)
        tail_u_ref[...] = jnp.zeros_like(tail_u_ref)

    @pl.when(j == 0)
    def _():
        h = h_ref[...]
        xn_ref[...] = _rms(h, g_ref[...]).astype(xn_ref.dtype)
        acc_ref[...] = h

    xn = xn_ref[...]
    for n in range(n_split):
        cols = slice(n * half, (n + 1) * half)
        for hh_ref, tail_ref, w_ref in ((hg_ref, tail_g_ref, wg_ref), (hu_ref, tail_u_ref, wu_ref)):
            hh_ref[n, 0:HALO, :] = tail_ref[j, n]
            hh = jnp.dot(xn, w_ref[:, cols], preferred_element_type=F32)
            hh_ref[n, HALO:HALO + tm, :] = hh
            tail_ref[j, n] = hh[tm - HALO:tm, :]

    def conv(hh_ref, n, c_ref, b_ref):
        cols = slice(n * half, (n + 1) * half)
        c = c_ref[:, cols]
        return (hh_ref[n, HALO:HALO + tm, :] * c[2:3]
                + hh_ref[n, HALO - 1:HALO - 1 + tm, :] * c[1:2]
                + hh_ref[n, HALO - 2:HALO - 2 + tm, :] * c[0:1] + b_ref[:, cols])

    for n in range(n_split):
        gate = conv(hg_ref, n, cg_ref, bg_ref)
        up = conv(hu_ref, n, cu_ref, bu_ref)
        act = (gate * jax.nn.sigmoid(gate) * up).astype(BF16)
        acc_ref[...] += jnp.dot(act, wo_ref[n * half:(n + 1) * half, :],
                                preferred_element_type=F32)

    @pl.when(j == pl.num_programs(1) - 1)
    def _():
        y_ref[...] = _rms(acc_ref[...], gf_ref[...]).astype(y_ref.dtype)


def _conv_ffn(h1, g_ffn, w_in, w_conv, b_conv, w_out, g_final, tm, tn):
    s, d = h1.shape
    f = w_out.shape[0]
    nb = f // tn
    n_split = 2
    half = tn // n_split
    return pl.pallas_call(
        functools.partial(_ffn_kernel, tm=tm, n_split=n_split),
        grid=(s // tm, nb),
        in_specs=[pl.BlockSpec((tm, d), lambda i, j: (i, 0)),
                  pl.BlockSpec((1, d), lambda i, j: (0, 0)),
                  pl.BlockSpec((d, tn), lambda i, j: (0, j)),
                  pl.BlockSpec((d, tn), lambda i, j: (0, j + nb)),
                  pl.BlockSpec((CONV_WIDTH, tn), lambda i, j: (0, j)),
                  pl.BlockSpec((CONV_WIDTH, tn), lambda i, j: (0, j + nb)),
                  pl.BlockSpec((1, tn), lambda i, j: (0, j)),
                  pl.BlockSpec((1, tn), lambda i, j: (0, j + nb)),
                  pl.BlockSpec((tn, d), lambda i, j: (j, 0)),
                  pl.BlockSpec((1, d), lambda i, j: (0, 0))],
        out_specs=pl.BlockSpec((tm, d), lambda i, j: (i, 0)),
        out_shape=jax.ShapeDtypeStruct((s, d), F32),
        scratch_shapes=[pltpu.VMEM((tm, d), BF16), pltpu.VMEM((tm, d), F32),
                        pltpu.VMEM((n_split, HALO + tm, half), F32),
                        pltpu.VMEM((n_split, HALO + tm, half), F32),
                        pltpu.VMEM((nb, n_split, HALO, half), F32),
                        pltpu.VMEM((nb, n_split, HALO, half), F32)],
        compiler_params=_params(2),
        name="conv_ffn",
    )(h1, g_ffn, w_in, w_in, w_conv, w_conv, b_conv, b_conv, w_out, g_final)


def _pad_cols(w, width):
    return jnp.pad(w, ((0, 0), (0, width - w.shape[1])))


def _layer(h, positions, norm_mix, w_in, b_forget, norm_q_a, w_uq, norm_kv_a, w_ukv,
           w_proj_mla, w_proj_fox, w_out, norm_ffn, w_ffn_in, w_conv, b_conv, w_ffn_out,
           final_gain):
    s = h.shape[0]
    tm = min(512, s)
    tm_mm = min(1024, s)
    hw = MLA_HEADS * LANES

    o = np.cumsum([0, MLA_Q_RANK, MLA_KV_RANK, MLA_ROPE_DIM, hw, hw, hw, FOX_HEADS,
                   D_MODEL, D_MODEL])
    w_lat = w_in[:, o[0]:o[2]]
    w_kr = w_in[:, o[2]:o[3]]
    w_kr_swapped = jnp.concatenate([w_kr[:, ROPE_HALF:], w_kr[:, :ROPE_HALF]], axis=1)
    w_small = jnp.concatenate(
        [w_lat, _pad_cols(w_kr, LANES), _pad_cols(w_kr_swapped, LANES),
         _pad_cols(w_in[:, o[6]:o[7]], LANES)], axis=1).astype(BF16)
    w_fq = w_in[:, o[3]:o[4]].astype(BF16)
    w_fk = w_in[:, o[4]:o[5]].astype(BF16)
    w_fvt = w_in[:, o[5]:o[6]].T.astype(BF16)
    w_gates = w_in[:, o[7]:o[9]].astype(BF16)

    uq = w_uq.reshape(MLA_Q_RANK, MLA_HEADS, MLA_NOPE_DIM + MLA_ROPE_DIM)
    w_qn = uq[:, :, :MLA_NOPE_DIM].reshape(MLA_Q_RANK, hw).astype(BF16)
    r1 = uq[:, :, MLA_NOPE_DIM:MLA_NOPE_DIM + ROPE_HALF]
    r2 = uq[:, :, MLA_NOPE_DIM + ROPE_HALF:]
    zpad = jnp.zeros((MLA_Q_RANK, MLA_HEADS, LANES - MLA_ROPE_DIM), w_uq.dtype)
    w_qa = jnp.concatenate([r1, r2, zpad], axis=2).reshape(MLA_Q_RANK, hw).astype(BF16)
    w_qb = jnp.concatenate([r2, r1, zpad], axis=2).reshape(MLA_Q_RANK, hw).astype(BF16)
    ukv = w_ukv.reshape(MLA_KV_RANK, MLA_HEADS, MLA_NOPE_DIM + MLA_V_DIM)
    w_kn = ukv[:, :, :MLA_NOPE_DIM].reshape(MLA_KV_RANK, hw).astype(BF16)
    w_vt = ukv[:, :, MLA_NOPE_DIM:].reshape(MLA_KV_RANK, hw).T.astype(BF16)

    xn, small = _norm_proj(h, norm_mix[None, :], w_small, tm)
    fvt = _matmul_t(w_fvt, xn, BF16, tm, name="proj_fox_v")
    gates = _matmul(xn, w_gates, BF16, tm_mm, 1024, epilogue=jax.nn.sigmoid, name="proj_gates")

    per_row = LANES // ROPE_HALF
    inv_freq = ROPE_THETA ** (-jnp.arange(0, MLA_ROPE_DIM, 2, dtype=F32) / MLA_ROPE_DIM)
    pos_dense = jnp.repeat(positions, ROPE_HALF).reshape(s // per_row, LANES)
    invf_dense = jnp.tile(inv_freq, per_row)[None, :]
    cos_d, sin_d, nsin_d = _rope_tables(pos_dense, invf_dense, min(512, s // per_row))
    cos, sin, nsin = (t.reshape(s, ROPE_HALF) for t in (cos_d, sin_d, nsin_d))
    zeros = jnp.zeros((s, LANES - MLA_ROPE_DIM), F32)
    ct = jnp.concatenate([cos, cos, zeros], axis=1)
    st = jnp.concatenate([nsin, sin, zeros], axis=1)

    q_mla, k_mla, vt_mla = _mla_prep(small, ct, st, norm_q_a[None, :], norm_kv_a[None, :],
                                     w_qn, w_qa, w_qb, w_kn, w_vt, tm)

    fl_col = (MLA_Q_RANK + MLA_KV_RANK) // LANES + 2
    cum = _decay_cumsum(small, fl_col, _pad_cols(b_forget[None, :], LANES), min(512, s))
    q_fox, k_fox = _fox_prep(xn, cum, w_fq, w_fk, tm)

    tq, tk = min(2048, s), 512
    o_mla = _attention(q_mla, k_mla, vt_mla, MLA_HEADS, MLA_V_DIM, tq, tk, "attn_mla")
    o_fox = _attention(q_fox, k_fox, fvt, FOX_HEADS, FOX_HEAD_DIM, tq, tk, "attn_fox")

    merged = _merge(o_mla, o_fox, w_proj_mla.astype(BF16), w_proj_fox.astype(BF16), gates,
                    tm_mm, 1024)
    h1 = _matmul(merged, w_out.astype(BF16), F32, tm, D_MODEL,
                 epilogue=lambda acc, res: acc + res, extras=(h,), name="proj_out")

    return _conv_ffn(h1, norm_ffn[None, :], w_ffn_in.astype(BF16), w_conv, b_conv[None, :],
                     w_ffn_out.astype(BF16), final_gain[None, :], tm, 512)


def kernel(x, positions, norm_mix, w_in, b_forget, norm_q_a, w_uq, norm_kv_a, w_ukv, w_proj_mla,
           w_proj_fox, w_out, norm_ffn, w_ffn_in, w_conv, b_conv, w_ffn_out, norm_final):
    batch, _, _ = x.shape
    depth = w_in.shape[0]
    assert depth == 1, "the fused ConvFFN folds the final norm into the only layer"
    outs = []
    for b in range(batch):
        y = _layer(x[b], positions[b], norm_mix[0], w_in[0], b_forget[0], norm_q_a[0], w_uq[0],
                   norm_kv_a[0], w_ukv[0], w_proj_mla[0], w_proj_fox[0], w_out[0], norm_ffn[0],
                   w_ffn_in[0], w_conv[0], b_conv[0], w_ffn_out[0], norm_final)
        outs.append(y[None])
    return outs[0] if batch == 1 else jnp.concatenate(outs, axis=0)
```

```python
import functools
import math

import jax
import jax.numpy as jnp
import numpy as np
from jax import lax
from jax.experimental import pallas as pl
from jax.experimental.pallas import tpu as pltpu

F32 = jnp.float32
BF16 = jnp.bfloat16

D_MODEL = 2048
MLA_HEADS = 8
MLA_Q_RANK = 512
MLA_KV_RANK = 512
MLA_NOPE_DIM = 128
MLA_ROPE_DIM = 64
MLA_V_DIM = 128
FOX_HEADS = 8
FOX_HEAD_DIM = 128
D_FF = 5632
CONV_WIDTH = 3
ROPE_THETA = 10000.0
NORM_EPS = 1e-6

LANES = 128
QK_WIDTH = 2 * LANES
ROPE_HALF = MLA_ROPE_DIM // 2
LOG2E = math.log2(math.e)
VMEM_LIMIT_BYTES = 56 * 1024 * 1024


def _params(n_grid_dims):
    return pltpu.CompilerParams(
        dimension_semantics=("arbitrary",) * n_grid_dims,
        vmem_limit_bytes=VMEM_LIMIT_BYTES)


def _rms(x, g):
    y = x * lax.rsqrt(jnp.mean(x * x, axis=-1, keepdims=True) + NORM_EPS)
    return y * g


def _norm_proj_kernel(x_ref, g_ref, w_ref, xn_ref, o_ref):
    xn = _rms(x_ref[...], g_ref[...]).astype(xn_ref.dtype)
    xn_ref[...] = xn
    o_ref[...] = jnp.dot(xn, w_ref[...], preferred_element_type=F32).astype(o_ref.dtype)


def _norm_proj(x, g, w, tm):
    s, d = x.shape
    n = w.shape[1]
    return pl.pallas_call(
        _norm_proj_kernel,
        grid=(s // tm,),
        in_specs=[pl.BlockSpec((tm, d), lambda i: (i, 0)),
                  pl.BlockSpec((1, d), lambda i: (0, 0)),
                  pl.BlockSpec((d, n), lambda i: (0, 0))],
        out_specs=[pl.BlockSpec((tm, d), lambda i: (i, 0)),
                   pl.BlockSpec((tm, n), lambda i: (i, 0))],
        out_shape=[jax.ShapeDtypeStruct((s, d), BF16), jax.ShapeDtypeStruct((s, n), F32)],
        compiler_params=_params(1),
        name="norm_proj_latents",
    )(x, g, w)


def _mm_kernel(a_ref, w_ref, *rest, epilogue):
    o_ref = rest[-1]
    acc = jnp.dot(a_ref[...], w_ref[...], preferred_element_type=F32)
    if epilogue is not None:
        acc = epilogue(acc, *[r[...] for r in rest[:-1]])
    o_ref[...] = acc.astype(o_ref.dtype)


def _matmul(a, w, out_dtype, tm, tn, epilogue=None, extras=(), name="matmul"):
    m, k = a.shape
    n = w.shape[1]
    in_specs = [pl.BlockSpec((tm, k), lambda i, j: (i, 0)),
                pl.BlockSpec((k, tn), lambda i, j: (0, j))]
    in_specs += [pl.BlockSpec((tm, tn), lambda i, j: (i, j)) for _ in extras]
    return pl.pallas_call(
        functools.partial(_mm_kernel, epilogue=epilogue),
        grid=(m // tm, n // tn),
        in_specs=in_specs,
        out_specs=pl.BlockSpec((tm, tn), lambda i, j: (i, j)),
        out_shape=jax.ShapeDtypeStruct((m, n), out_dtype),
        compiler_params=_params(2),
        name=name,
    )(a, w, *extras)


def _mm_t_kernel(wt_ref, a_ref, o_ref):
    o_ref[...] = lax.dot_general(wt_ref[...], a_ref[...], (((1,), (1,)), ((), ())),
                                 preferred_element_type=F32).astype(o_ref.dtype)


def _matmul_t(wt, a, out_dtype, tm, name):
    n, k = wt.shape
    m = a.shape[0]
    return pl.pallas_call(
        _mm_t_kernel,
        grid=(m // tm,),
        in_specs=[pl.BlockSpec((n, k), lambda i: (0, 0)),
                  pl.BlockSpec((tm, k), lambda i: (i, 0))],
        out_specs=pl.BlockSpec((n, tm), lambda i: (0, i)),
        out_shape=jax.ShapeDtypeStruct((n, m), out_dtype),
        compiler_params=_params(1),
        name=name,
    )(wt, a)


def _rope_table_kernel(pos_ref, invf_ref, cos_ref, sin_ref, nsin_ref):
    ang = pos_ref[...].astype(F32) * invf_ref[...]
    s = jnp.sin(ang)
    cos_ref[...] = jnp.cos(ang)
    sin_ref[...] = s
    nsin_ref[...] = -s


def _rope_tables(pos_dense, invf_dense, tb):
    r = pos_dense.shape[0]
    spec = pl.BlockSpec((tb, LANES), lambda i: (i, 0))
    return pl.pallas_call(
        _rope_table_kernel,
        grid=(r // tb,),
        in_specs=[spec, pl.BlockSpec((1, LANES), lambda i: (0, 0))],
        out_specs=[spec, spec, spec],
        out_shape=[jax.ShapeDtypeStruct((r, LANES), F32)] * 3,
        compiler_params=_params(1),
        name="rope_tables",
    )(pos_dense, invf_dense)


def _split3(x):
    p1 = x.astype(BF16)
    r1 = x - p1.astype(F32)
    p2 = r1.astype(BF16)
    p3 = (r1 - p2.astype(F32)).astype(BF16)
    return p1, p2, p3


def _decay_kernel(fl_ref, b_ref, o_ref, carry_ref, *, tb):
    @pl.when(pl.program_id(0) == 0)
    def _():
        carry_ref[...] = jnp.zeros_like(carry_ref)

    z = fl_ref[...] + b_ref[...]
    logf = jnp.minimum(z, 0.0) - jnp.log1p(jnp.exp(-jnp.abs(z)))
    row = lax.broadcasted_iota(jnp.int32, (tb, tb), 0)
    col = lax.broadcasted_iota(jnp.int32, (tb, tb), 1)
    tri = (col <= row).astype(BF16)
    c = carry_ref[...]
    for piece in _split3(logf):
        c = c + jnp.dot(tri, piece, preferred_element_type=F32)
    o_ref[...] = c * LOG2E
    carry_ref[...] = c[tb - 1:tb, :]


def _decay_cumsum(small, fl_col, b_tile, tb):
    s = small.shape[0]
    return pl.pallas_call(
        functools.partial(_decay_kernel, tb=tb),
        grid=(s // tb,),
        in_specs=[pl.BlockSpec((tb, LANES), lambda i: (i, fl_col)),
                  pl.BlockSpec((1, LANES), lambda i: (0, 0))],
        out_specs=pl.BlockSpec((tb, LANES), lambda i: (i, 0)),
        out_shape=jax.ShapeDtypeStruct((s, LANES), F32),
        scratch_shapes=[pltpu.VMEM((1, LANES), F32)],
        compiler_params=_params(1),
        name="decay_cumsum",
    )(small, b_tile)


def _mla_prep_kernel(cq_ref, ckv_ref, ka_ref, kb_ref, ct_ref, st_ref, gq_ref, gkv_ref,
                     wqn_ref, wqa_ref, wqb_ref, wkn_ref, wvt_ref,
                     q_ref, k_ref, vt_ref, *, q_scale):
    ct = ct_ref[...]
    st = st_ref[...]
    cq = _rms(cq_ref[...], gq_ref[...]).astype(BF16)
    ckv = _rms(ckv_ref[...], gkv_ref[...]).astype(BF16)
    qn = jnp.dot(cq, wqn_ref[...], preferred_element_type=F32)
    qa = jnp.dot(cq, wqa_ref[...], preferred_element_type=F32)
    qb = jnp.dot(cq, wqb_ref[...], preferred_element_type=F32)
    kn = jnp.dot(ckv, wkn_ref[...], preferred_element_type=F32)
    vt_ref[...] = lax.dot_general(wvt_ref[...], ckv, (((1,), (1,)), ((), ())),
                                  preferred_element_type=F32).astype(vt_ref.dtype)
    kr = (ka_ref[...] * ct + kb_ref[...] * st).astype(k_ref.dtype)
    for h in range(MLA_HEADS):
        lo, hi = h * LANES, (h + 1) * LANES
        base = h * QK_WIDTH
        q_ref[:, base:base + LANES] = (qn[:, lo:hi] * q_scale).astype(q_ref.dtype)
        q_ref[:, base + LANES:base + QK_WIDTH] = (
            (qa[:, lo:hi] * ct + qb[:, lo:hi] * st) * q_scale).astype(q_ref.dtype)
        k_ref[:, base:base + LANES] = kn[:, lo:hi].astype(k_ref.dtype)
        k_ref[:, base + LANES:base + QK_WIDTH] = kr


def _mla_prep(small, ct, st, gq, gkv, wqn, wqa, wqb, wkn, wvt, tm):
    s = small.shape[0]
    hw = MLA_HEADS * LANES
    full = lambda shape: pl.BlockSpec(shape, lambda i: (0, 0))
    rows = lambda width, col: pl.BlockSpec((tm, width), lambda i, c=col: (i, c))
    q_scale = (MLA_NOPE_DIM + MLA_ROPE_DIM) ** -0.5 * LOG2E
    return pl.pallas_call(
        functools.partial(_mla_prep_kernel, q_scale=q_scale),
        grid=(s // tm,),
        in_specs=[rows(MLA_Q_RANK, 0), rows(MLA_KV_RANK, 1),
                  rows(LANES, (MLA_Q_RANK + MLA_KV_RANK) // LANES),
                  rows(LANES, (MLA_Q_RANK + MLA_KV_RANK) // LANES + 1),
                  rows(LANES, 0), rows(LANES, 0),
                  full((1, MLA_Q_RANK)), full((1, MLA_KV_RANK)),
                  full((MLA_Q_RANK, hw)), full((MLA_Q_RANK, hw)), full((MLA_Q_RANK, hw)),
                  full((MLA_KV_RANK, hw)), full((hw, MLA_KV_RANK))],
        out_specs=[pl.BlockSpec((tm, MLA_HEADS * QK_WIDTH), lambda i: (i, 0)),
                   pl.BlockSpec((tm, MLA_HEADS * QK_WIDTH), lambda i: (i, 0)),
                   pl.BlockSpec((hw, tm), lambda i: (0, i))],
        out_shape=[jax.ShapeDtypeStruct((s, MLA_HEADS * QK_WIDTH), BF16),
                   jax.ShapeDtypeStruct((s, MLA_HEADS * QK_WIDTH), BF16),
                   jax.ShapeDtypeStruct((hw, s), BF16)],
        compiler_params=_params(1),
        name="mla_prep",
    )(small, small, small, small, ct, st, gq, gkv, wqn, wqa, wqb, wkn, wvt)


def _fox_prep_kernel(xn_ref, cum_ref, wq_ref, wk_ref, place_ref, ones_ref,
                     q_ref, k_ref, *, q_scale):
    hw = FOX_HEADS * LANES
    xn = xn_ref[...]
    fq = jnp.dot(xn, wq_ref[...], preferred_element_type=F32) * q_scale
    fk = jnp.dot(xn, wk_ref[...], preferred_element_type=F32)
    pieces = jnp.concatenate(_split3(cum_ref[...]), axis=1)
    bias = jnp.dot(pieces, place_ref[...], preferred_element_type=F32) + ones_ref[...]
    qb, kb = bias[:, :hw], bias[:, hw:]
    for h in range(FOX_HEADS):
        lo, hi = h * LANES, (h + 1) * LANES
        base = h * QK_WIDTH
        q_ref[:, base:base + LANES] = fq[:, lo:hi].astype(q_ref.dtype)
        q_ref[:, base + LANES:base + QK_WIDTH] = qb[:, lo:hi].astype(q_ref.dtype)
        k_ref[:, base:base + LANES] = fk[:, lo:hi].astype(k_ref.dtype)
        k_ref[:, base + LANES:base + QK_WIDTH] = kb[:, lo:hi].astype(k_ref.dtype)


def _fox_bias_constants():
    hw = FOX_HEADS * LANES
    place = np.zeros((3 * LANES, 2 * hw), np.float32)
    ones = np.zeros((1, 2 * hw), np.float32)
    for h in range(FOX_HEADS):
        for p in range(3):
            place[p * LANES + h, h * LANES + p] = 1.0
            ones[0, h * LANES + 3 + p] = 1.0
            place[p * LANES + h, hw + h * LANES + 3 + p] = -1.0
            ones[0, hw + h * LANES + p] = 1.0
    return jnp.asarray(place, BF16), jnp.asarray(ones)


def _fox_prep(xn, cum, wq, wk, tm):
    s, d = xn.shape
    hw = FOX_HEADS * LANES
    place, ones = _fox_bias_constants()
    q_scale = FOX_HEAD_DIM ** -0.5 * LOG2E
    return pl.pallas_call(
        functools.partial(_fox_prep_kernel, q_scale=q_scale),
        grid=(s // tm,),
        in_specs=[pl.BlockSpec((tm, d), lambda i: (i, 0)),
                  pl.BlockSpec((tm, LANES), lambda i: (i, 0)),
                  pl.BlockSpec((d, hw), lambda i: (0, 0)),
                  pl.BlockSpec((d, hw), lambda i: (0, 0)),
                  pl.BlockSpec((3 * LANES, 2 * hw), lambda i: (0, 0)),
                  pl.BlockSpec((1, 2 * hw), lambda i: (0, 0))],
        out_specs=[pl.BlockSpec((tm, FOX_HEADS * QK_WIDTH), lambda i: (i, 0))] * 2,
        out_shape=[jax.ShapeDtypeStruct((s, FOX_HEADS * QK_WIDTH), BF16)] * 2,
        compiler_params=_params(1),
        name="fox_prep",
    )(xn, cum, wq, wk, place, ones)


def _attn_kernel(q_ref, k_ref, vt_ref, o_ref, m_ref, l_ref, acc_ref, sa_ref, sb_ref,
                 mxa_ref, mxb_ref, *, tq, tk):
    i = pl.program_id(1)
    ncol = tq // tk
    m_ref[...] = jnp.full_like(m_ref, -jnp.inf)
    l_ref[...] = jnp.zeros_like(l_ref)
    acc_ref[...] = jnp.zeros_like(acc_ref)

    def logits(start, groups, buf):
        s_ref, mx_ref = buf
        k = k_ref[pl.ds(pl.multiple_of(start, tk), tk), :]
        for c in groups:
            st = lax.dot_general(k, q_ref[c * tk:(c + 1) * tk, :],
                                 (((1,), (1,)), ((), ())), preferred_element_type=F32)
            s_ref[c] = st
            mx_ref[c] = jnp.max(st, axis=0, keepdims=True)

    def softmax_pv(start, groups, diag_group, buf):
        s_ref, mx_ref = buf
        vt = vt_ref[:, pl.ds(pl.multiple_of(start, tk), tk)]
        for c in groups:
            cols = slice(c * tk, (c + 1) * tk)
            st = s_ref[c]
            if c == diag_group:
                kv_pos = lax.broadcasted_iota(jnp.int32, (tk, tk), 0)
                q_pos = lax.broadcasted_iota(jnp.int32, (tk, tk), 1)
                st = jnp.where(kv_pos <= q_pos, st, -jnp.inf)
                tile_max = jnp.max(st, axis=0, keepdims=True)
            else:
                tile_max = mx_ref[c]
            m_prev = m_ref[:, cols]
            m_new = jnp.maximum(m_prev, tile_max)
            alpha = jnp.exp2(m_prev - m_new)
            pt = jnp.exp2(st - m_new)
            l_ref[:, cols] = alpha * l_ref[:, cols] + jnp.sum(pt, axis=0, keepdims=True)
            acc_ref[:, cols] = alpha * acc_ref[:, cols] + jnp.dot(
                vt, pt.astype(BF16), preferred_element_type=F32)
            m_ref[:, cols] = m_new

    every = list(range(ncol))
    n_full = i * ncol
    buf_a, buf_b = (sa_ref, mxa_ref), (sb_ref, mxb_ref)
    logits(0, every, buf_a)

    def tile_pair(u, carry):
        t0 = 2 * u
        logits((t0 + 1) * tk, every, buf_b)
        softmax_pv(t0 * tk, every, None, buf_a)
        logits((t0 + 2) * tk, every, buf_a)
        softmax_pv((t0 + 1) * tk, every, None, buf_b)
        return carry

    lax.fori_loop(0, n_full // 2, tile_pair, 0)
    bufs = (buf_a, buf_b)
    base = n_full * tk
    for r in range(ncol):
        if r + 1 < ncol:
            logits(base + (r + 1) * tk, list(range(r + 1, ncol)), bufs[(r + 1) % 2])
        softmax_pv(base + r * tk, list(range(r, ncol)), r, bufs[r % 2])
    for c in range(ncol):
        cols = slice(c * tk, (c + 1) * tk)
        o_ref[cols, :] = (acc_ref[:, cols] / l_ref[:, cols]).T.astype(o_ref.dtype)


def _attention(q, k, vt, n_heads, dv, tq, tk, name):
    s = q.shape[0]
    ncol = tq // tk
    assert tq % tk == 0 and ncol % 2 == 0, "the kv pipeline assumes an even tile count per q block"
    return pl.pallas_call(
        functools.partial(_attn_kernel, tq=tq, tk=tk),
        grid=(n_heads, s // tq),
        in_specs=[pl.BlockSpec((tq, QK_WIDTH), lambda h, i: (i, h)),
                  pl.BlockSpec((s, QK_WIDTH), lambda h, i: (0, h)),
                  pl.BlockSpec((dv, s), lambda h, i: (h, 0), pipeline_mode=pl.Buffered(1))],
        out_specs=pl.BlockSpec((tq, dv), lambda h, i: (i, h)),
        out_shape=jax.ShapeDtypeStruct((s, n_heads * dv), BF16),
        scratch_shapes=[pltpu.VMEM((1, tq), F32), pltpu.VMEM((1, tq), F32),
                        pltpu.VMEM((dv, tq), F32),
                        pltpu.VMEM((ncol, tk, tk), F32), pltpu.VMEM((ncol, tk, tk), F32),
                        pltpu.VMEM((ncol, 1, tk), F32), pltpu.VMEM((ncol, 1, tk), F32)],
        compiler_params=_params(2),
        name=name,
    )(q, k, vt)


def _merge_kernel(om_ref, of_ref, wm_ref, wf_ref, gm_ref, gf_ref, o_ref):
    pm = jnp.dot(om_ref[...], wm_ref[...], preferred_element_type=F32)
    pf = jnp.dot(of_ref[...], wf_ref[...], preferred_element_type=F32)
    o_ref[...] = (gm_ref[...].astype(F32) * pm + gf_ref[...].astype(F32) * pf).astype(o_ref.dtype)


def _merge(o_mla, o_fox, wm, wf, gates, tm, tn):
    s, kd = o_mla.shape
    n = wm.shape[1]
    nb = n // tn
    return pl.pallas_call(
        _merge_kernel,
        grid=(s // tm, nb),
        in_specs=[pl.BlockSpec((tm, kd), lambda i, j: (i, 0)),
                  pl.BlockSpec((tm, kd), lambda i, j: (i, 0)),
                  pl.BlockSpec((kd, tn), lambda i, j: (0, j)),
                  pl.BlockSpec((kd, tn), lambda i, j: (0, j)),
                  pl.BlockSpec((tm, tn), lambda i, j: (i, j)),
                  pl.BlockSpec((tm, tn), lambda i, j: (i, j + nb))],
        out_specs=pl.BlockSpec((tm, tn), lambda i, j: (i, j)),
        out_shape=jax.ShapeDtypeStruct((s, n), BF16),
        compiler_params=_params(2),
        name="gated_merge",
    )(o_mla, o_fox, wm, wf, gates, gates)


HALO = 8


def _ffn_kernel(h_ref, g_ref, wg_ref, wu_ref, cg_ref, cu_ref, bg_ref, bu_ref,
                wo_ref, gf_ref, y_ref, xn_ref, acc_ref, hg_ref, hu_ref, tail_g_ref, tail_u_ref,
                *, tm, n_split):
    i = pl.program_id(0)
    j = pl.program_id(1)
    half = wg_ref.shape[1] // n_split

    @pl.when(jnp.logical_and(i == 0, j == 0))
    def _():
        tail_g_ref[...] = jnp.zeros_like(tail_g_ref)
        tail_u_ref[...] = jnp.zeros_like(tail_u_ref)

    @pl.when(j == 0)
    def _():
        h = h_ref[...]
        xn_ref[...] = _rms(h, g_ref[...]).astype(xn_ref.dtype)
        acc_ref[...] = h

    xn = xn_ref[...]
    for n in range(n_split):
        cols = slice(n * half, (n + 1) * half)
        for hh_ref, tail_ref, w_ref in ((hg_ref, tail_g_ref, wg_ref), (hu_ref, tail_u_ref, wu_ref)):
            hh_ref[n, 0:HALO, :] = tail_ref[j, n]
            hh = jnp.dot(xn, w_ref[:, cols], preferred_element_type=F32)
            hh_ref[n, HALO:HALO + tm, :] = hh
            tail_ref[j, n] = hh[tm - HALO:tm, :]

    def conv(hh_ref, n, c_ref, b_ref):
        cols = slice(n * half, (n + 1) * half)
        c = c_ref[:, cols]
        return (hh_ref[n, HALO:HALO + tm, :] * c[2:3]
                + hh_ref[n, HALO - 1:HALO - 1 + tm, :] * c[1:2]
                + hh_ref[n, HALO - 2:HALO - 2 + tm, :] * c[0:1] + b_ref[:, cols])

    for n in range(n_split):
        gate = conv(hg_ref, n, cg_ref, bg_ref)
        up = conv(hu_ref, n, cu_ref, bu_ref)
        act = (gate * jax.nn.sigmoid(gate) * up).astype(BF16)
        acc_ref[...] += jnp.dot(act, wo_ref[n * half:(n + 1) * half, :],
                                preferred_element_type=F32)

    @pl.when(j == pl.num_programs(1) - 1)
    def _():
        y_ref[...] = _rms(acc_ref[...], gf_ref[...]).astype(y_ref.dtype)


def _conv_ffn(h1, g_ffn, w_in, w_conv, b_conv, w_out, g_final, tm, tn):
    s, d = h1.shape
    f = w_out.shape[0]
    nb = f // tn
    n_split = 2
    half = tn // n_split
    return pl.pallas_call(
        functools.partial(_ffn_kernel, tm=tm, n_split=n_split),
        grid=(s // tm, nb),
        in_specs=[pl.BlockSpec((tm, d), lambda i, j: (i, 0)),
                  pl.BlockSpec((1, d), lambda i, j: (0, 0)),
                  pl.BlockSpec((d, tn), lambda i, j: (0, j)),
                  pl.BlockSpec((d, tn), lambda i, j: (0, j + nb)),
                  pl.BlockSpec((CONV_WIDTH, tn), lambda i, j: (0, j)),
                  pl.BlockSpec((CONV_WIDTH, tn), lambda i, j: (0, j + nb)),
                  pl.BlockSpec((1, tn), lambda i, j: (0, j)),
                  pl.BlockSpec((1, tn), lambda i, j: (0, j + nb)),
                  pl.BlockSpec((tn, d), lambda i, j: (j, 0)),
                  pl.BlockSpec((1, d), lambda i, j: (0, 0))],
        out_specs=pl.BlockSpec((tm, d), lambda i, j: (i, 0)),
        out_shape=jax.ShapeDtypeStruct((s, d), F32),
        scratch_shapes=[pltpu.VMEM((tm, d), BF16), pltpu.VMEM((tm, d), F32),
                        pltpu.VMEM((n_split, HALO + tm, half), F32),
                        pltpu.VMEM((n_split, HALO + tm, half), F32),
                        pltpu.VMEM((nb, n_split, HALO, half), F32),
                        pltpu.VMEM((nb, n_split, HALO, half), F32)],
        compiler_params=_params(2),
        name="conv_ffn",
    )(h1, g_ffn, w_in, w_in, w_conv, w_conv, b_conv, b_conv, w_out, g_final)


def _pad_cols(w, width):
    return jnp.pad(w, ((0, 0), (0, width - w.shape[1])))


def _layer(h, positions, norm_mix, w_in, b_forget, norm_q_a, w_uq, norm_kv_a, w_ukv,
           w_proj_mla, w_proj_fox, w_out, norm_ffn, w_ffn_in, w_conv, b_conv, w_ffn_out,
           final_gain):
    s = h.shape[0]
    tm = min(512, s)
    tm_mm = min(1024, s)
    hw = MLA_HEADS * LANES

    o = np.cumsum([0, MLA_Q_RANK, MLA_KV_RANK, MLA_ROPE_DIM, hw, hw, hw, FOX_HEADS,
                   D_MODEL, D_MODEL])
    w_lat = w_in[:, o[0]:o[2]]
    w_kr = w_in[:, o[2]:o[3]]
    w_kr_swapped = jnp.concatenate([w_kr[:, ROPE_HALF:], w_kr[:, :ROPE_HALF]], axis=1)
    w_small = jnp.concatenate(
        [w_lat, _pad_cols(w_kr, LANES), _pad_cols(w_kr_swapped, LANES),
         _pad_cols(w_in[:, o[6]:o[7]], LANES)], axis=1).astype(BF16)
    w_fq = w_in[:, o[3]:o[4]].astype(BF16)
    w_fk = w_in[:, o[4]:o[5]].astype(BF16)
    w_fvt = w_in[:, o[5]:o[6]].T.astype(BF16)
    w_gates = w_in[:, o[7]:o[9]].astype(BF16)

    uq = w_uq.reshape(MLA_Q_RANK, MLA_HEADS, MLA_NOPE_DIM + MLA_ROPE_DIM)
    w_qn = uq[:, :, :MLA_NOPE_DIM].reshape(MLA_Q_RANK, hw).astype(BF16)
    r1 = uq[:, :, MLA_NOPE_DIM:MLA_NOPE_DIM + ROPE_HALF]
    r2 = uq[:, :, MLA_NOPE_DIM + ROPE_HALF:]
    zpad = jnp.zeros((MLA_Q_RANK, MLA_HEADS, LANES - MLA_ROPE_DIM), w_uq.dtype)
    w_qa = jnp.concatenate([r1, r2, zpad], axis=2).reshape(MLA_Q_RANK, hw).astype(BF16)
    w_qb = jnp.concatenate([r2, r1, zpad], axis=2).reshape(MLA_Q_RANK, hw).astype(BF16)
    ukv = w_ukv.reshape(MLA_KV_RANK, MLA_HEADS, MLA_NOPE_DIM + MLA_V_DIM)
    w_kn = ukv[:, :, :MLA_NOPE_DIM].reshape(MLA_KV_RANK, hw).astype(BF16)
    w_vt = ukv[:, :, MLA_NOPE_DIM:].reshape(MLA_KV_RANK, hw).T.astype(BF16)

    xn, small = _norm_proj(h, norm_mix[None, :], w_small, tm)
    fvt = _matmul_t(w_fvt, xn, BF16, tm_mm, name="proj_fox_v")
    gates = _matmul(xn, w_gates, BF16, min(2048, s), 1024, epilogue=jax.nn.sigmoid,
                    name="proj_gates")

    per_row = LANES // ROPE_HALF
    inv_freq = ROPE_THETA ** (-jnp.arange(0, MLA_ROPE_DIM, 2, dtype=F32) / MLA_ROPE_DIM)
    pos_dense = jnp.repeat(positions, ROPE_HALF).reshape(s // per_row, LANES)
    invf_dense = jnp.tile(inv_freq, per_row)[None, :]
    cos_d, sin_d, nsin_d = _rope_tables(pos_dense, invf_dense, min(512, s // per_row))
    cos, sin, nsin = (t.reshape(s, ROPE_HALF) for t in (cos_d, sin_d, nsin_d))
    zeros = jnp.zeros((s, LANES - MLA_ROPE_DIM), F32)
    ct = jnp.concatenate([cos, cos, zeros], axis=1)
    st = jnp.concatenate([nsin, sin, zeros], axis=1)

    q_mla, k_mla, vt_mla = _mla_prep(small, ct, st, norm_q_a[None, :], norm_kv_a[None, :],
                                     w_qn, w_qa, w_qb, w_kn, w_vt, tm_mm)

    fl_col = (MLA_Q_RANK + MLA_KV_RANK) // LANES + 2
    cum = _decay_cumsum(small, fl_col, _pad_cols(b_forget[None, :], LANES), min(512, s))
    q_fox, k_fox = _fox_prep(xn, cum, w_fq, w_fk, tm_mm)

    tq, tk = min(4096, s), 512
    o_mla = _attention(q_mla, k_mla, vt_mla, MLA_HEADS, MLA_V_DIM, tq, tk, "attn_mla")
    o_fox = _attention(q_fox, k_fox, fvt, FOX_HEADS, FOX_HEAD_DIM, tq, tk, "attn_fox")

    merged = _merge(o_mla, o_fox, w_proj_mla.astype(BF16), w_proj_fox.astype(BF16), gates,
                    tm_mm, 1024)
    h1 = _matmul(merged, w_out.astype(BF16), F32, tm, D_MODEL,
                 epilogue=lambda acc, res: acc + res, extras=(h,), name="proj_out")

    return _conv_ffn(h1, norm_ffn[None, :], w_ffn_in.astype(BF16), w_conv, b_conv[None, :],
                     w_ffn_out.astype(BF16), final_gain[None, :], tm, 512)


def kernel(x, positions, norm_mix, w_in, b_forget, norm_q_a, w_uq, norm_kv_a, w_ukv, w_proj_mla,
           w_proj_fox, w_out, norm_ffn, w_ffn_in, w_conv, b_conv, w_ffn_out, norm_final):
    batch, _, _ = x.shape
    depth = w_in.shape[0]
    assert depth == 1, "the fused ConvFFN folds the final norm into the only layer"
    outs = []
    for b in range(batch):
        y = _layer(x[b], positions[b], norm_mix[0], w_in[0], b_forget[0], norm_q_a[0], w_uq[0],
                   norm_kv_a[0], w_ukv[0], w_proj_mla[0], w_proj_fox[0], w_out[0], norm_ffn[0],
                   w_ffn_in[0], w_conv[0], b_conv[0], w_ffn_out[0], norm_final)
        outs.append(y[None])
    return outs[0] if batch == 1 else jnp.concatenate(outs, axis=0)
```

```python
import functools
import math

import jax
import jax.numpy as jnp
import numpy as np
from jax import lax
from jax.experimental import pallas as pl
from jax.experimental.pallas import tpu as pltpu

F32 = jnp.float32
BF16 = jnp.bfloat16

D_MODEL = 2048
MLA_HEADS = 8
MLA_Q_RANK = 512
MLA_KV_RANK = 512
MLA_NOPE_DIM = 128
MLA_ROPE_DIM = 64
MLA_V_DIM = 128
FOX_HEADS = 8
FOX_HEAD_DIM = 128
D_FF = 5632
CONV_WIDTH = 3
ROPE_THETA = 10000.0
NORM_EPS = 1e-6

LANES = 128
QK_WIDTH = 2 * LANES
ROPE_HALF = MLA_ROPE_DIM // 2
LOG2E = math.log2(math.e)
VMEM_LIMIT_BYTES = 56 * 1024 * 1024


def _params(n_grid_dims):
    return pltpu.CompilerParams(
        dimension_semantics=("arbitrary",) * n_grid_dims,
        vmem_limit_bytes=VMEM_LIMIT_BYTES)


def _rms(x, g):
    y = x * lax.rsqrt(jnp.mean(x * x, axis=-1, keepdims=True) + NORM_EPS)
    return y * g


def _norm_proj_kernel(x_ref, g_ref, w_ref, xn_ref, o_ref):
    xn = _rms(x_ref[...], g_ref[...]).astype(xn_ref.dtype)
    xn_ref[...] = xn
    o_ref[...] = jnp.dot(xn, w_ref[...], preferred_element_type=F32).astype(o_ref.dtype)


def _norm_proj(x, g, w, tm):
    s, d = x.shape
    n = w.shape[1]
    return pl.pallas_call(
        _norm_proj_kernel,
        grid=(s // tm,),
        in_specs=[pl.BlockSpec((tm, d), lambda i: (i, 0)),
                  pl.BlockSpec((1, d), lambda i: (0, 0)),
                  pl.BlockSpec((d, n), lambda i: (0, 0))],
        out_specs=[pl.BlockSpec((tm, d), lambda i: (i, 0)),
                   pl.BlockSpec((tm, n), lambda i: (i, 0))],
        out_shape=[jax.ShapeDtypeStruct((s, d), BF16), jax.ShapeDtypeStruct((s, n), F32)],
        compiler_params=_params(1),
        name="norm_proj_latents",
    )(x, g, w)


def _mm_kernel(a_ref, w_ref, *rest, epilogue):
    o_ref = rest[-1]
    acc = jnp.dot(a_ref[...], w_ref[...], preferred_element_type=F32)
    if epilogue is not None:
        acc = epilogue(acc, *[r[...] for r in rest[:-1]])
    o_ref[...] = acc.astype(o_ref.dtype)


def _matmul(a, w, out_dtype, tm, tn, epilogue=None, extras=(), name="matmul"):
    m, k = a.shape
    n = w.shape[1]
    in_specs = [pl.BlockSpec((tm, k), lambda i, j: (i, 0)),
                pl.BlockSpec((k, tn), lambda i, j: (0, j))]
    in_specs += [pl.BlockSpec((tm, tn), lambda i, j: (i, j)) for _ in extras]
    return pl.pallas_call(
        functools.partial(_mm_kernel, epilogue=epilogue),
        grid=(m // tm, n // tn),
        in_specs=in_specs,
        out_specs=pl.BlockSpec((tm, tn), lambda i, j: (i, j)),
        out_shape=jax.ShapeDtypeStruct((m, n), out_dtype),
        compiler_params=_params(2),
        name=name,
    )(a, w, *extras)


def _out_proj_kernel(a_ref, w_ref, x_ref, g_ref, h_ref, hn_ref):
    h = x_ref[...] + jnp.dot(a_ref[...], w_ref[...], preferred_element_type=F32)
    h_ref[...] = h
    hn_ref[...] = _rms(h, g_ref[...]).astype(hn_ref.dtype)


def _out_proj(a, w, x, g, tm):
    s, k = a.shape
    d = w.shape[1]
    row = lambda width: pl.BlockSpec((tm, width), lambda i: (i, 0))
    return pl.pallas_call(
        _out_proj_kernel,
        grid=(s // tm,),
        in_specs=[row(k), pl.BlockSpec((k, d), lambda i: (0, 0)), row(d),
                  pl.BlockSpec((1, d), lambda i: (0, 0))],
        out_specs=[row(d), row(d)],
        out_shape=[jax.ShapeDtypeStruct((s, d), F32), jax.ShapeDtypeStruct((s, d), BF16)],
        compiler_params=_params(1),
        name="proj_out",
    )(a, w, x, g)


def _mm_t_kernel(wt_ref, a_ref, o_ref):
    o_ref[...] = lax.dot_general(wt_ref[...], a_ref[...], (((1,), (1,)), ((), ())),
                                 preferred_element_type=F32).astype(o_ref.dtype)


def _matmul_t(wt, a, out_dtype, tm, name):
    n, k = wt.shape
    m = a.shape[0]
    return pl.pallas_call(
        _mm_t_kernel,
        grid=(m // tm,),
        in_specs=[pl.BlockSpec((n, k), lambda i: (0, 0)),
                  pl.BlockSpec((tm, k), lambda i: (i, 0))],
        out_specs=pl.BlockSpec((n, tm), lambda i: (0, i)),
        out_shape=jax.ShapeDtypeStruct((n, m), out_dtype),
        compiler_params=_params(1),
        name=name,
    )(wt, a)


def _rope_table_kernel(pos_ref, invf_ref, cos_ref, sin_ref, nsin_ref):
    ang = pos_ref[...].astype(F32) * invf_ref[...]
    s = jnp.sin(ang)
    cos_ref[...] = jnp.cos(ang)
    sin_ref[...] = s
    nsin_ref[...] = -s


def _rope_tables(pos_dense, invf_dense, tb):
    r = pos_dense.shape[0]
    spec = pl.BlockSpec((tb, LANES), lambda i: (i, 0))
    return pl.pallas_call(
        _rope_table_kernel,
        grid=(r // tb,),
        in_specs=[spec, pl.BlockSpec((1, LANES), lambda i: (0, 0))],
        out_specs=[spec, spec, spec],
        out_shape=[jax.ShapeDtypeStruct((r, LANES), F32)] * 3,
        compiler_params=_params(1),
        name="rope_tables",
    )(pos_dense, invf_dense)


def _split3(x):
    p1 = x.astype(BF16)
    r1 = x - p1.astype(F32)
    p2 = r1.astype(BF16)
    p3 = (r1 - p2.astype(F32)).astype(BF16)
    return p1, p2, p3


def _decay_kernel(fl_ref, b_ref, o_ref, carry_ref, *, tb):
    @pl.when(pl.program_id(0) == 0)
    def _():
        carry_ref[...] = jnp.zeros_like(carry_ref)

    z = fl_ref[...] + b_ref[...]
    logf = jnp.minimum(z, 0.0) - jnp.log1p(jnp.exp(-jnp.abs(z)))
    row = lax.broadcasted_iota(jnp.int32, (tb, tb), 0)
    col = lax.broadcasted_iota(jnp.int32, (tb, tb), 1)
    tri = (col <= row).astype(BF16)
    c = carry_ref[...]
    for piece in _split3(logf):
        c = c + jnp.dot(tri, piece, preferred_element_type=F32)
    o_ref[...] = c * LOG2E
    carry_ref[...] = c[tb - 1:tb, :]


def _decay_cumsum(small, fl_col, b_tile, tb):
    s = small.shape[0]
    return pl.pallas_call(
        functools.partial(_decay_kernel, tb=tb),
        grid=(s // tb,),
        in_specs=[pl.BlockSpec((tb, LANES), lambda i: (i, fl_col)),
                  pl.BlockSpec((1, LANES), lambda i: (0, 0))],
        out_specs=pl.BlockSpec((tb, LANES), lambda i: (i, 0)),
        out_shape=jax.ShapeDtypeStruct((s, LANES), F32),
        scratch_shapes=[pltpu.VMEM((1, LANES), F32)],
        compiler_params=_params(1),
        name="decay_cumsum",
    )(small, b_tile)


def _mla_prep_kernel(cq_ref, ckv_ref, ka_ref, kb_ref, ct_ref, st_ref, gq_ref, gkv_ref,
                     wqn_ref, wqa_ref, wqb_ref, wkn_ref, wvt_ref,
                     q_ref, k_ref, vt_ref, *, q_scale):
    ct = ct_ref[...]
    st = st_ref[...]
    cq = _rms(cq_ref[...], gq_ref[...]).astype(BF16)
    ckv = _rms(ckv_ref[...], gkv_ref[...]).astype(BF16)
    qn = jnp.dot(cq, wqn_ref[...], preferred_element_type=F32)
    qa = jnp.dot(cq, wqa_ref[...], preferred_element_type=F32)
    qb = jnp.dot(cq, wqb_ref[...], preferred_element_type=F32)
    kn = jnp.dot(ckv, wkn_ref[...], preferred_element_type=F32)
    vt_ref[...] = lax.dot_general(wvt_ref[...], ckv, (((1,), (1,)), ((), ())),
                                  preferred_element_type=F32).astype(vt_ref.dtype)
    kr = (ka_ref[...] * ct + kb_ref[...] * st).astype(k_ref.dtype)
    for h in range(MLA_HEADS):
        lo, hi = h * LANES, (h + 1) * LANES
        base = h * QK_WIDTH
        q_ref[:, base:base + LANES] = (qn[:, lo:hi] * q_scale).astype(q_ref.dtype)
        q_ref[:, base + LANES:base + QK_WIDTH] = (
            (qa[:, lo:hi] * ct + qb[:, lo:hi] * st) * q_scale).astype(q_ref.dtype)
        k_ref[:, base:base + LANES] = kn[:, lo:hi].astype(k_ref.dtype)
        k_ref[:, base + LANES:base + QK_WIDTH] = kr


def _mla_prep(small, ct, st, gq, gkv, wqn, wqa, wqb, wkn, wvt, tm):
    s = small.shape[0]
    hw = MLA_HEADS * LANES
    full = lambda shape: pl.BlockSpec(shape, lambda i: (0, 0))
    rows = lambda width, col: pl.BlockSpec((tm, width), lambda i, c=col: (i, c))
    q_scale = (MLA_NOPE_DIM + MLA_ROPE_DIM) ** -0.5 * LOG2E
    return pl.pallas_call(
        functools.partial(_mla_prep_kernel, q_scale=q_scale),
        grid=(s // tm,),
        in_specs=[rows(MLA_Q_RANK, 0), rows(MLA_KV_RANK, 1),
                  rows(LANES, (MLA_Q_RANK + MLA_KV_RANK) // LANES),
                  rows(LANES, (MLA_Q_RANK + MLA_KV_RANK) // LANES + 1),
                  rows(LANES, 0), rows(LANES, 0),
                  full((1, MLA_Q_RANK)), full((1, MLA_KV_RANK)),
                  full((MLA_Q_RANK, hw)), full((MLA_Q_RANK, hw)), full((MLA_Q_RANK, hw)),
                  full((MLA_KV_RANK, hw)), full((hw, MLA_KV_RANK))],
        out_specs=[pl.BlockSpec((tm, MLA_HEADS * QK_WIDTH), lambda i: (i, 0)),
                   pl.BlockSpec((tm, MLA_HEADS * QK_WIDTH), lambda i: (i, 0)),
                   pl.BlockSpec((hw, tm), lambda i: (0, i))],
        out_shape=[jax.ShapeDtypeStruct((s, MLA_HEADS * QK_WIDTH), BF16),
                   jax.ShapeDtypeStruct((s, MLA_HEADS * QK_WIDTH), BF16),
                   jax.ShapeDtypeStruct((hw, s), BF16)],
        compiler_params=_params(1),
        name="mla_prep",
    )(small, small, small, small, ct, st, gq, gkv, wqn, wqa, wqb, wkn, wvt)


def _fox_prep_kernel(xn_ref, cum_ref, wq_ref, wk_ref, place_ref, ones_ref,
                     q_ref, k_ref, *, q_scale):
    hw = FOX_HEADS * LANES
    xn = xn_ref[...]
    fq = jnp.dot(xn, wq_ref[...], preferred_element_type=F32) * q_scale
    fk = jnp.dot(xn, wk_ref[...], preferred_element_type=F32)
    pieces = jnp.concatenate(_split3(cum_ref[...]), axis=1)
    bias = jnp.dot(pieces, place_ref[...], preferred_element_type=F32) + ones_ref[...]
    qb, kb = bias[:, :hw], bias[:, hw:]
    for h in range(FOX_HEADS):
        lo, hi = h * LANES, (h + 1) * LANES
        base = h * QK_WIDTH
        q_ref[:, base:base + LANES] = fq[:, lo:hi].astype(q_ref.dtype)
        q_ref[:, base + LANES:base + QK_WIDTH] = qb[:, lo:hi].astype(q_ref.dtype)
        k_ref[:, base:base + LANES] = fk[:, lo:hi].astype(k_ref.dtype)
        k_ref[:, base + LANES:base + QK_WIDTH] = kb[:, lo:hi].astype(k_ref.dtype)


def _fox_bias_constants():
    hw = FOX_HEADS * LANES
    place = np.zeros((3 * LANES, 2 * hw), np.float32)
    ones = np.zeros((1, 2 * hw), np.float32)
    for h in range(FOX_HEADS):
        for p in range(3):
            place[p * LANES + h, h * LANES + p] = 1.0
            ones[0, h * LANES + 3 + p] = 1.0
            place[p * LANES + h, hw + h * LANES + 3 + p] = -1.0
            ones[0, hw + h * LANES + p] = 1.0
    return jnp.asarray(place, BF16), jnp.asarray(ones)


def _fox_prep(xn, cum, wq, wk, tm):
    s, d = xn.shape
    hw = FOX_HEADS * LANES
    place, ones = _fox_bias_constants()
    q_scale = FOX_HEAD_DIM ** -0.5 * LOG2E
    return pl.pallas_call(
        functools.partial(_fox_prep_kernel, q_scale=q_scale),
        grid=(s // tm,),
        in_specs=[pl.BlockSpec((tm, d), lambda i: (i, 0)),
                  pl.BlockSpec((tm, LANES), lambda i: (i, 0)),
                  pl.BlockSpec((d, hw), lambda i: (0, 0)),
                  pl.BlockSpec((d, hw), lambda i: (0, 0)),
                  pl.BlockSpec((3 * LANES, 2 * hw), lambda i: (0, 0)),
                  pl.BlockSpec((1, 2 * hw), lambda i: (0, 0))],
        out_specs=[pl.BlockSpec((tm, FOX_HEADS * QK_WIDTH), lambda i: (i, 0))] * 2,
        out_shape=[jax.ShapeDtypeStruct((s, FOX_HEADS * QK_WIDTH), BF16)] * 2,
        compiler_params=_params(1),
        name="fox_prep",
    )(xn, cum, wq, wk, place, ones)


def _attn_kernel(q_ref, k_ref, vt_ref, o_ref, m_ref, l_ref, acc_ref, sa_ref, sb_ref,
                 mxa_ref, mxb_ref, *, tq, tk):
    i = pl.program_id(1)
    ncol = tq // tk
    m_ref[...] = jnp.full_like(m_ref, -jnp.inf)
    l_ref[...] = jnp.zeros_like(l_ref)
    acc_ref[...] = jnp.zeros_like(acc_ref)

    def logits(start, groups, buf):
        s_ref, mx_ref = buf
        k = k_ref[pl.ds(pl.multiple_of(start, tk), tk), :]
        for c in groups:
            st = lax.dot_general(k, q_ref[c * tk:(c + 1) * tk, :],
                                 (((1,), (1,)), ((), ())), preferred_element_type=F32)
            s_ref[c] = st
            mx_ref[c] = jnp.max(st, axis=0, keepdims=True)

    def softmax_pv(start, groups, diag_group, buf):
        s_ref, mx_ref = buf
        vt = vt_ref[:, pl.ds(pl.multiple_of(start, tk), tk)]
        for c in groups:
            cols = slice(c * tk, (c + 1) * tk)
            st = s_ref[c]
            if c == diag_group:
                kv_pos = lax.broadcasted_iota(jnp.int32, (tk, tk), 0)
                q_pos = lax.broadcasted_iota(jnp.int32, (tk, tk), 1)
                st = jnp.where(kv_pos <= q_pos, st, -jnp.inf)
                tile_max = jnp.max(st, axis=0, keepdims=True)
            else:
                tile_max = mx_ref[c]
            m_prev = m_ref[:, cols]
            m_new = jnp.maximum(m_prev, tile_max)
            alpha = jnp.exp2(m_prev - m_new)
            pt = jnp.exp2(st - m_new)
            l_ref[:, cols] = alpha * l_ref[:, cols] + jnp.sum(pt, axis=0, keepdims=True)
            acc_ref[:, cols] = alpha * acc_ref[:, cols] + jnp.dot(
                vt, pt.astype(BF16), preferred_element_type=F32)
            m_ref[:, cols] = m_new

    every = list(range(ncol))
    n_full = i * ncol
    buf_a, buf_b = (sa_ref, mxa_ref), (sb_ref, mxb_ref)
    logits(0, every, buf_a)

    def tile_pair(u, carry):
        t0 = 2 * u
        logits((t0 + 1) * tk, every, buf_b)
        softmax_pv(t0 * tk, every, None, buf_a)
        logits((t0 + 2) * tk, every, buf_a)
        softmax_pv((t0 + 1) * tk, every, None, buf_b)
        return carry

    lax.fori_loop(0, n_full // 2, tile_pair, 0)
    bufs = (buf_a, buf_b)
    base = n_full * tk
    for r in range(ncol):
        if r + 1 < ncol:
            logits(base + (r + 1) * tk, list(range(r + 1, ncol)), bufs[(r + 1) % 2])
        softmax_pv(base + r * tk, list(range(r, ncol)), r, bufs[r % 2])
    for c in range(ncol):
        cols = slice(c * tk, (c + 1) * tk)
        o_ref[cols, :] = (acc_ref[:, cols] / l_ref[:, cols]).T.astype(o_ref.dtype)


def _attention(q, k, vt, n_heads, dv, tq, tk, name):
    s = q.shape[0]
    ncol = tq // tk
    assert tq % tk == 0 and ncol % 2 == 0, "the kv pipeline assumes an even tile count per q block"
    return pl.pallas_call(
        functools.partial(_attn_kernel, tq=tq, tk=tk),
        grid=(n_heads, s // tq),
        in_specs=[pl.BlockSpec((tq, QK_WIDTH), lambda h, i: (i, h)),
                  pl.BlockSpec((s, QK_WIDTH), lambda h, i: (0, h)),
                  pl.BlockSpec((dv, s), lambda h, i: (h, 0), pipeline_mode=pl.Buffered(1))],
        out_specs=pl.BlockSpec((tq, dv), lambda h, i: (i, h)),
        out_shape=jax.ShapeDtypeStruct((s, n_heads * dv), BF16),
        scratch_shapes=[pltpu.VMEM((1, tq), F32), pltpu.VMEM((1, tq), F32),
                        pltpu.VMEM((dv, tq), F32),
                        pltpu.VMEM((ncol, tk, tk), F32), pltpu.VMEM((ncol, tk, tk), F32),
                        pltpu.VMEM((ncol, 1, tk), F32), pltpu.VMEM((ncol, 1, tk), F32)],
        compiler_params=_params(2),
        name=name,
    )(q, k, vt)


def _merge_kernel(om_ref, of_ref, wm_ref, wf_ref, gm_ref, gf_ref, o_ref):
    pm = jnp.dot(om_ref[...], wm_ref[...], preferred_element_type=F32)
    pf = jnp.dot(of_ref[...], wf_ref[...], preferred_element_type=F32)
    o_ref[...] = (gm_ref[...].astype(F32) * pm + gf_ref[...].astype(F32) * pf).astype(o_ref.dtype)


def _merge(o_mla, o_fox, wm, wf, gates, tm, tn):
    s, kd = o_mla.shape
    n = wm.shape[1]
    nb = n // tn
    return pl.pallas_call(
        _merge_kernel,
        grid=(s // tm, nb),
        in_specs=[pl.BlockSpec((tm, kd), lambda i, j: (i, 0)),
                  pl.BlockSpec((tm, kd), lambda i, j: (i, 0)),
                  pl.BlockSpec((kd, tn), lambda i, j: (0, j)),
                  pl.BlockSpec((kd, tn), lambda i, j: (0, j)),
                  pl.BlockSpec((tm, tn), lambda i, j: (i, j)),
                  pl.BlockSpec((tm, tn), lambda i, j: (i, j + nb))],
        out_specs=pl.BlockSpec((tm, tn), lambda i, j: (i, j)),
        out_shape=jax.ShapeDtypeStruct((s, n), BF16),
        compiler_params=_params(2),
        name="gated_merge",
    )(o_mla, o_fox, wm, wf, gates, gates)


HALO = 8


def _ffn_kernel(h_ref, xn_ref, wg_ref, wu_ref, cg_ref, cu_ref, bg_ref, bu_ref,
                wo_ref, gf_ref, y_ref, acc_ref, hg_ref, hu_ref, tail_g_ref, tail_u_ref,
                *, tm, n_split):
    i = pl.program_id(0)
    j = pl.program_id(1)
    half = wg_ref.shape[1] // n_split

    @pl.when(jnp.logical_and(i == 0, j == 0))
    def _():
        tail_g_ref[...] = jnp.zeros_like(tail_g_ref)
        tail_u_ref[...] = jnp.zeros_like(tail_u_ref)

    @pl.when(j == 0)
    def _():
        acc_ref[...] = h_ref[...]

    xn = xn_ref[...]
    for n in range(n_split):
        cols = slice(n * half, (n + 1) * half)
        for hh_ref, tail_ref, w_ref in ((hg_ref, tail_g_ref, wg_ref), (hu_ref, tail_u_ref, wu_ref)):
            hh_ref[n, 0:HALO, :] = tail_ref[j, n]
            hh = jnp.dot(xn, w_ref[:, cols], preferred_element_type=F32)
            hh_ref[n, HALO:HALO + tm, :] = hh
            tail_ref[j, n] = hh[tm - HALO:tm, :]

    def conv(hh_ref, n, c_ref, b_ref):
        cols = slice(n * half, (n + 1) * half)
        c = c_ref[:, cols]
        return (hh_ref[n, HALO:HALO + tm, :] * c[2:3]
                + hh_ref[n, HALO - 1:HALO - 1 + tm, :] * c[1:2]
                + hh_ref[n, HALO - 2:HALO - 2 + tm, :] * c[0:1] + b_ref[:, cols])

    for n in range(n_split):
        gate = conv(hg_ref, n, cg_ref, bg_ref)
        up = conv(hu_ref, n, cu_ref, bu_ref)
        act = (gate * jax.nn.sigmoid(gate) * up).astype(BF16)
        acc_ref[...] += jnp.dot(act, wo_ref[n * half:(n + 1) * half, :],
                                preferred_element_type=F32)

    @pl.when(j == pl.num_programs(1) - 1)
    def _():
        y_ref[...] = _rms(acc_ref[...], gf_ref[...]).astype(y_ref.dtype)


def _conv_ffn(h1, h1n, w_in, w_conv, b_conv, w_out, g_final, tm, tn):
    s, d = h1.shape
    f = w_out.shape[0]
    nb = f // tn
    n_split = 2
    half = tn // n_split
    return pl.pallas_call(
        functools.partial(_ffn_kernel, tm=tm, n_split=n_split),
        grid=(s // tm, nb),
        in_specs=[pl.BlockSpec((tm, d), lambda i, j: (i, 0)),
                  pl.BlockSpec((tm, d), lambda i, j: (i, 0)),
                  pl.BlockSpec((d, tn), lambda i, j: (0, j)),
                  pl.BlockSpec((d, tn), lambda i, j: (0, j + nb)),
                  pl.BlockSpec((CONV_WIDTH, tn), lambda i, j: (0, j)),
                  pl.BlockSpec((CONV_WIDTH, tn), lambda i, j: (0, j + nb)),
                  pl.BlockSpec((1, tn), lambda i, j: (0, j)),
                  pl.BlockSpec((1, tn), lambda i, j: (0, j + nb)),
                  pl.BlockSpec((tn, d), lambda i, j: (j, 0)),
                  pl.BlockSpec((1, d), lambda i, j: (0, 0))],
        out_specs=pl.BlockSpec((tm, d), lambda i, j: (i, 0)),
        out_shape=jax.ShapeDtypeStruct((s, d), F32),
        scratch_shapes=[pltpu.VMEM((tm, d), F32),
                        pltpu.VMEM((n_split, HALO + tm, half), F32),
                        pltpu.VMEM((n_split, HALO + tm, half), F32),
                        pltpu.VMEM((nb, n_split, HALO, half), F32),
                        pltpu.VMEM((nb, n_split, HALO, half), F32)],
        compiler_params=_params(2),
        name="conv_ffn",
    )(h1, h1n, w_in, w_in, w_conv, w_conv, b_conv, b_conv, w_out, g_final)


def _pad_cols(w, width):
    return jnp.pad(w, ((0, 0), (0, width - w.shape[1])))


def _layer(h, positions, norm_mix, w_in, b_forget, norm_q_a, w_uq, norm_kv_a, w_ukv,
           w_proj_mla, w_proj_fox, w_out, norm_ffn, w_ffn_in, w_conv, b_conv, w_ffn_out,
           final_gain):
    s = h.shape[0]
    tm = min(512, s)
    tm_mm = min(1024, s)
    hw = MLA_HEADS * LANES

    o = np.cumsum([0, MLA_Q_RANK, MLA_KV_RANK, MLA_ROPE_DIM, hw, hw, hw, FOX_HEADS,
                   D_MODEL, D_MODEL])
    w_lat = w_in[:, o[0]:o[2]]
    w_kr = w_in[:, o[2]:o[3]]
    w_kr_swapped = jnp.concatenate([w_kr[:, ROPE_HALF:], w_kr[:, :ROPE_HALF]], axis=1)
    w_small = jnp.concatenate(
        [w_lat, _pad_cols(w_kr, LANES), _pad_cols(w_kr_swapped, LANES),
         _pad_cols(w_in[:, o[6]:o[7]], LANES)], axis=1).astype(BF16)
    w_fq = w_in[:, o[3]:o[4]].astype(BF16)
    w_fk = w_in[:, o[4]:o[5]].astype(BF16)
    w_fvt = w_in[:, o[5]:o[6]].T.astype(BF16)
    w_gates = w_in[:, o[7]:o[9]].astype(BF16)

    uq = w_uq.reshape(MLA_Q_RANK, MLA_HEADS, MLA_NOPE_DIM + MLA_ROPE_DIM)
    w_qn = uq[:, :, :MLA_NOPE_DIM].reshape(MLA_Q_RANK, hw).astype(BF16)
    r1 = uq[:, :, MLA_NOPE_DIM:MLA_NOPE_DIM + ROPE_HALF]
    r2 = uq[:, :, MLA_NOPE_DIM + ROPE_HALF:]
    zpad = jnp.zeros((MLA_Q_RANK, MLA_HEADS, LANES - MLA_ROPE_DIM), w_uq.dtype)
    w_qa = jnp.concatenate([r1, r2, zpad], axis=2).reshape(MLA_Q_RANK, hw).astype(BF16)
    w_qb = jnp.concatenate([r2, r1, zpad], axis=2).reshape(MLA_Q_RANK, hw).astype(BF16)
    ukv = w_ukv.reshape(MLA_KV_RANK, MLA_HEADS, MLA_NOPE_DIM + MLA_V_DIM)
    w_kn = ukv[:, :, :MLA_NOPE_DIM].reshape(MLA_KV_RANK, hw).astype(BF16)
    w_vt = ukv[:, :, MLA_NOPE_DIM:].reshape(MLA_KV_RANK, hw).T.astype(BF16)

    xn, small = _norm_proj(h, norm_mix[None, :], w_small, tm)
    fvt = _matmul_t(w_fvt, xn, BF16, tm_mm, name="proj_fox_v")
    gates = _matmul(xn, w_gates, BF16, min(2048, s), 1024, epilogue=jax.nn.sigmoid,
                    name="proj_gates")

    per_row = LANES // ROPE_HALF
    inv_freq = ROPE_THETA ** (-jnp.arange(0, MLA_ROPE_DIM, 2, dtype=F32) / MLA_ROPE_DIM)
    pos_dense = jnp.repeat(positions, ROPE_HALF).reshape(s // per_row, LANES)
    invf_dense = jnp.tile(inv_freq, per_row)[None, :]
    cos_d, sin_d, nsin_d = _rope_tables(pos_dense, invf_dense, min(512, s // per_row))
    cos, sin, nsin = (t.reshape(s, ROPE_HALF) for t in (cos_d, sin_d, nsin_d))
    zeros = jnp.zeros((s, LANES - MLA_ROPE_DIM), F32)
    ct = jnp.concatenate([cos, cos, zeros], axis=1)
    st = jnp.concatenate([nsin, sin, zeros], axis=1)

    q_mla, k_mla, vt_mla = _mla_prep(small, ct, st, norm_q_a[None, :], norm_kv_a[None, :],
                                     w_qn, w_qa, w_qb, w_kn, w_vt, tm_mm)

    fl_col = (MLA_Q_RANK + MLA_KV_RANK) // LANES + 2
    cum = _decay_cumsum(small, fl_col, _pad_cols(b_forget[None, :], LANES), min(512, s))
    q_fox, k_fox = _fox_prep(xn, cum, w_fq, w_fk, tm_mm)

    tq, tk = min(4096, s), 512
    o_mla = _attention(q_mla, k_mla, vt_mla, MLA_HEADS, MLA_V_DIM, tq, tk, "attn_mla")
    o_fox = _attention(q_fox, k_fox, fvt, FOX_HEADS, FOX_HEAD_DIM, tq, tk, "attn_fox")

    merged = _merge(o_mla, o_fox, w_proj_mla.astype(BF16), w_proj_fox.astype(BF16), gates,
                    tm_mm, 1024)
    h1, h1n = _out_proj(merged, w_out.astype(BF16), h, norm_ffn[None, :], tm)

    return _conv_ffn(h1, h1n, w_ffn_in.astype(BF16), w_conv, b_conv[None, :],
                     w_ffn_out.astype(BF16), final_gain[None, :], tm, 512)


def kernel(x, positions, norm_mix, w_in, b_forget, norm_q_a, w_uq, norm_kv_a, w_ukv, w_proj_mla,
           w_proj_fox, w_out, norm_ffn, w_ffn_in, w_conv, b_conv, w_ffn_out, norm_final):
    batch, _, _ = x.shape
    depth = w_in.shape[0]
    assert depth == 1, "the fused ConvFFN folds the final norm into the only layer"
    outs = []
    for b in range(batch):
        y = _layer(x[b], positions[b], norm_mix[0], w_in[0], b_forget[0], norm_q_a[0], w_uq[0],
                   norm_kv_a[0], w_ukv[0], w_proj_mla[0], w_proj_fox[0], w_out[0], norm_ffn[0],
                   w_ffn_in[0], w_conv[0], b_conv[0], w_ffn_out[0], norm_final)
        outs.append(y[None])
    return outs[0] if batch == 1 else jnp.concatenate(outs, axis=0)
```
